```python
import math
import jax, jax.numpy as jnp
from jax import lax
import numpy as np

D_MODEL = 2048
BATCH = 8
SEQ = 2048
DEPTH = 1
DEC_BATCH = 128
DEC_SEQ = 1
PAST_LEN = 16384
PAGE_SIZE = 128

N_META = 16
EPS = 1e-6
D_INNER = 2 * D_MODEL
SSM_HEAD_DIM = 64
SSM_HEADS = D_INNER // SSM_HEAD_DIM
SSM_GROUPS = 8
SSM_STATE = 128
CONV_W = 4
CONV_CH = D_INNER + 2 * SSM_GROUPS * SSM_STATE
CHUNK = 128
MLA_HEADS = D_MODEL // 128
QK_NOPE = 128
QK_ROPE = 64
QK_HEAD = QK_NOPE + QK_ROPE
V_DIM = 128
KV_LORA = D_MODEL // 4
ROPE_THETA = 10000.0
Q_BLOCK = 128
QK_SCALE = QK_HEAD ** -0.5
FFN_HIDDEN = -(-8 * D_MODEL // (3 * 256)) * 256

kernel_name = 'hybrid_ssd_mla_gated_decoder_step'


def _rmsnorm(x, w):
    xf = x.astype(jnp.float32)
    y = xf * lax.rsqrt(jnp.mean(xf * xf, axis=-1, keepdims=True) + EPS)
    return (y * w.astype(jnp.float32)).astype(x.dtype)


def _rope(x, pos):
    half = QK_ROPE // 2
    inv = ROPE_THETA ** (-jnp.arange(half, dtype=jnp.float32) / half)
    ang = pos.astype(jnp.float32)[:, None] * inv[None, :]
    cos = jnp.cos(ang)[None, :, None, :]
    sin = jnp.sin(ang)[None, :, None, :]
    xf = x.astype(jnp.float32)
    x1, x2 = xf[..., :half], xf[..., half:]
    return jnp.concatenate([x1 * cos - x2 * sin, x2 * cos + x1 * sin], axis=-1).astype(x.dtype)


def _project(h, w_norm, w_in):
    cols = (D_INNER, CONV_CH, SSM_HEADS, MLA_HEADS * QK_HEAD, KV_LORA, QK_ROPE, D_MODEL, D_MODEL)
    cuts = [int(c) for c in np.cumsum(cols)[:-1]]
    return jnp.split(_rmsnorm(h, w_norm) @ w_in, cuts, axis=-1)


def _causal_dwconv(u, prev, w, b):
    t = u.shape[1]
    full = jnp.concatenate([prev.astype(u.dtype), u], axis=1)
    out = b
    for k in range(CONV_W):
        out = out + w[k] * full[:, k:k + t]
    return jax.nn.silu(out), full[:, t:]


def _ssm_pre(xbc_raw, dt_raw, conv_prev, conv_w, conv_b, dt_bias, a_log):
    xbc, conv_state = _causal_dwconv(xbc_raw, conv_prev, conv_w, conv_b)
    b, t = xbc.shape[:2]
    gn = SSM_GROUPS * SSM_STATE
    xs = xbc[..., :D_INNER].reshape(b, t, SSM_HEADS, SSM_HEAD_DIM)
    bm = xbc[..., D_INNER:D_INNER + gn].reshape(b, t, SSM_GROUPS, SSM_STATE)
    cm = xbc[..., D_INNER + gn:].reshape(b, t, SSM_GROUPS, SSM_STATE)
    dt = jax.nn.softplus(dt_raw.astype(jnp.float32) + dt_bias.astype(jnp.float32))
    da = dt * (-jnp.exp(a_log.astype(jnp.float32)))
    return xs, dt, da, bm, cm, conv_state


def _ssd_block(h0, xs, dt, da, bm, cm):
    b, l = xs.shape[:2]
    e = SSM_HEADS // SSM_GROUPS
    cs = jnp.cumsum(da, axis=1)
    causal = jnp.tril(jnp.ones((l, l), dtype=bool))[None, :, :, None]
    decay = jnp.exp(jnp.where(causal, cs[:, :, None, :] - cs[:, None, :, :], -jnp.inf))
    decay = decay.reshape(b, l, l, SSM_GROUPS, e)
    xdt = (xs * dt[..., None]).reshape(b, l, SSM_GROUPS, e, SSM_HEAD_DIM)
    cb = jnp.einsum('btgn,bsgn->btsg', cm, bm)
    y_diag = jnp.einsum('btsg,btsge,bsgep->btgep', cb, decay, xdt)
    hg = h0.reshape(b, SSM_GROUPS, e, SSM_HEAD_DIM, SSM_STATE)
    y_off = jnp.einsum('btgn,bgepn->btgep', cm, hg) * jnp.exp(cs).reshape(b, l, SSM_GROUPS, e)[..., None]
    to_end = jnp.exp(cs[:, -1:] - cs).reshape(b, l, SSM_GROUPS, e)
    h_new = (hg * jnp.exp(cs[:, -1]).reshape(b, SSM_GROUPS, e)[..., None, None]
             + jnp.einsum('bsgn,bsge,bsgep->bgepn', bm, to_end, xdt))
    y = (y_diag + y_off).reshape(b, l, SSM_HEADS, SSM_HEAD_DIM)
    return y, h_new.reshape(b, SSM_HEADS, SSM_HEAD_DIM, SSM_STATE)


def _ssd_prompt(xs, dt, da, bm, cm):
    b, t = xs.shape[:2]
    lead = CHUNK - N_META
    n_chunks = (lead + t) // CHUNK

    def chunks(a):
        a = jnp.pad(a, [(0, 0), (lead, 0)] + [(0, 0)] * (a.ndim - 2))
        return a.reshape((b, n_chunks, CHUNK) + a.shape[2:]).swapaxes(0, 1)

    def step(h, inp):
        y, h_new = _ssd_block(h, *inp)
        return h_new, y

    h0 = jnp.zeros((b, SSM_HEADS, SSM_HEAD_DIM, SSM_STATE), jnp.float32)
    h_last, ys = lax.scan(step, h0, (chunks(xs), chunks(dt), chunks(da), chunks(bm), chunks(cm)))
    y = ys.swapaxes(0, 1).reshape((b, n_chunks * CHUNK, SSM_HEADS, SSM_HEAD_DIM))[:, lead:]
    return y, h_last


def _ssm_post(y, xs, z, d_skip, w_norm, w_o):
    b, t = z.shape[:2]
    y = (y + d_skip.astype(jnp.float32)[:, None] * xs).reshape(b, t, D_INNER) * jax.nn.silu(z.astype(jnp.float32))
    yg = y.reshape(b, t, SSM_GROUPS, D_INNER // SSM_GROUPS)
    yg = yg * lax.rsqrt(jnp.mean(yg * yg, axis=-1, keepdims=True) + EPS)
    y = yg.reshape(b, t, D_INNER) * w_norm.astype(jnp.float32)
    return y.astype(z.dtype) @ w_o


def _mla_qk(q_raw, ckv_raw, krope_raw, pos, w_kv_norm, q_norm, k_norm, w_kv_up):
    b, t = q_raw.shape[:2]
    q = _rmsnorm(q_raw.reshape(b, t, MLA_HEADS, QK_HEAD), q_norm)
    q_nope = q[..., :QK_NOPE]
    q_pe = _rope(q[..., QK_NOPE:], pos)
    ckv = _rmsnorm(ckv_raw, w_kv_norm)
    k_nope = jnp.einsum('btc,chd->bthd', ckv, w_kv_up[..., :QK_NOPE])
    kn32 = k_nope.astype(jnp.float32)
    kr32 = krope_raw.astype(jnp.float32)
    ssq = jnp.sum(kn32 * kn32, axis=-1) + jnp.sum(kr32 * kr32, axis=-1)[..., None]
    k_s = lax.rsqrt(ssq / QK_HEAD + EPS)
    k_pe = _rope((krope_raw * k_norm[QK_NOPE:])[:, :, None, :], pos)[:, :, 0]
    return q_nope, q_pe, ckv, k_nope, k_pe, k_s


def _mla_prompt(q_nope, q_pe, ckv, k_nope, k_pe, k_s, k_norm, w_kv_up):
    b, t = q_nope.shape[:2]
    k = jnp.concatenate([k_nope * k_norm[:QK_NOPE],
                         jnp.broadcast_to(k_pe[:, :, None, :], (b, t, MLA_HEADS, QK_ROPE))], axis=-1) * k_s[..., None]
    q = jnp.concatenate([q_nope, q_pe], axis=-1)
    v = jnp.einsum('btc,chd->bthd', ckv, w_kv_up[..., QK_NOPE:])
    n_q = -(-t // Q_BLOCK)
    tp = n_q * Q_BLOCK
    qb = jnp.pad(q, [(0, 0), (0, tp - t), (0, 0), (0, 0)]).reshape(b, n_q, Q_BLOCK, MLA_HEADS, QK_HEAD).swapaxes(0, 1)
    kpos = jnp.arange(t)

    def block(args):
        qi, start = args
        s = jnp.einsum('bqhd,bkhd->bhqk', qi, k).astype(jnp.float32) * QK_SCALE
        qpos = start + jnp.arange(Q_BLOCK)
        s = jnp.where(kpos[None, :] <= qpos[:, None], s, -jnp.inf)
        p = jax.nn.softmax(s, axis=-1)
        return jnp.einsum('bhqk,bkhd->bqhd', p.astype(v.dtype), v)

    o = lax.map(block, (qb, jnp.arange(n_q) * Q_BLOCK))
    return o.swapaxes(0, 1).reshape(b, tp, MLA_HEADS, V_DIM)[:, :t]


def _mla_sample(q_nope, q_pe, ckv_n, kpe_n, ks_n, cache_ckv, cache_kpe, cache_kscale, layer, page_table, k_norm, w_kv_up):
    t = q_nope.shape[1]
    q_abs = jnp.einsum('bthd,chd->bthc', q_nope * k_norm[:QK_NOPE], w_kv_up[..., :QK_NOPE]).astype(jnp.float32)
    q_r = q_pe.astype(jnp.float32)

    def scores(c, kp, ks):
        sc = jnp.einsum('bthc,bsc->bths', q_abs, c) + jnp.einsum('bthr,bsr->bths', q_r, kp)
        return sc.astype(jnp.float32) * ks.astype(jnp.float32).swapaxes(1, 2)[:, None] * QK_SCALE

    def update(carry, sc, c):
        m, l, acc = carry
        m_new = jnp.maximum(m, jnp.max(sc, axis=-1))
        corr = jnp.exp(m - m_new)
        p = jnp.exp(sc - m_new[..., None])
        acc = acc * corr[..., None] + jnp.einsum('bths,bsc->bthc', p, c.astype(jnp.float32))
        return (m_new, l * corr + jnp.sum(p, axis=-1), acc)

    def page_step(carry, phys):
        c = cache_ckv[layer, phys]
        kp = cache_kpe[layer, phys]
        ks = cache_kscale[layer, phys]
        return update(carry, scores(c, kp, ks), c), None

    b = q_nope.shape[0]
    init = (jnp.full((b, t, MLA_HEADS), -jnp.inf, jnp.float32),
            jnp.zeros((b, t, MLA_HEADS), jnp.float32),
            jnp.zeros((b, t, MLA_HEADS, KV_LORA), jnp.float32))
    carry, _ = lax.scan(page_step, init, page_table.T)
    sc = scores(ckv_n, kpe_n, ks_n)
    causal = jnp.arange(t)[:, None] >= jnp.arange(t)[None, :]
    sc = jnp.where(causal[None, :, None, :], sc, -jnp.inf)
    _, l, acc = update(carry, sc, ckv_n)
    o = acc / l[..., None]
    return jnp.einsum('bthc,chd->bthd', o, w_kv_up[..., QK_NOPE:].astype(jnp.float32))


def _merge_ffn(h, att, ssm_out, ga, gs, w_attn_out, w_out, w_norm_ffn, w_ffn_in, w_ffn_out):
    b, t = h.shape[:2]
    attn_out = att.reshape(b, t, MLA_HEADS * V_DIM).astype(h.dtype) @ w_attn_out
    merged = jax.nn.sigmoid(ga) * attn_out + jax.nn.sigmoid(gs) * ssm_out.astype(h.dtype)
    h = h + merged @ w_out
    gate, up = jnp.split(_rmsnorm(h, w_norm_ffn) @ w_ffn_in, 2, axis=-1)
    return h + (jax.nn.silu(gate) * up) @ w_ffn_out


def setup_inputs(seed: int = 0) -> dict:
    key = jax.random.key(seed)
    k = jax.random.split(key, 32)
    f32 = jnp.float32
    n_pages = PAST_LEN // PAGE_SIZE
    n_used = DEC_BATCH * n_pages
    n_pool = n_used + n_used // 4

    def nrm(kk, shape, scale=1.0):
        return scale * jax.random.normal(kk, shape, f32)

    def gain(kk, n):
        return 1.0 + 0.02 * jax.random.normal(kk, (DEPTH, n), f32)

    page_table = jax.random.permutation(k[0], n_pool)[:n_used].reshape(DEC_BATCH, n_pages).astype(jnp.int32)
    dt0 = jnp.exp(jax.random.uniform(k[1], (DEPTH, SSM_HEADS), f32, math.log(1e-3), math.log(1e-1)))
    dt_bias = dt0 + jnp.log(-jnp.expm1(-dt0))
    a_log = jnp.log(jax.random.uniform(k[2], (DEPTH, SSM_HEADS), f32, 1.0, 16.0))
    in_cols = D_INNER + CONV_CH + SSM_HEADS + MLA_HEADS * QK_HEAD + KV_LORA + QK_ROPE + 2 * D_MODEL
    return {
        'x_prompt': nrm(k[3], (BATCH, SEQ, D_MODEL)),
        'x_sample': nrm(k[4], (DEC_BATCH, DEC_SEQ, D_MODEL)),
        'cache_ckv': nrm(k[5], (DEPTH, n_pool, PAGE_SIZE, KV_LORA)),
        'cache_kpe': nrm(k[6], (DEPTH, n_pool, PAGE_SIZE, QK_ROPE)),
        'cache_kscale': jax.random.uniform(k[7], (DEPTH, n_pool, PAGE_SIZE, MLA_HEADS), f32, 0.5, 1.5),
        'page_table': page_table,
        'state_ssm': nrm(k[8], (DEPTH, DEC_BATCH, SSM_HEADS, SSM_HEAD_DIM, SSM_STATE), 0.1),
        'state_conv': nrm(k[9], (DEPTH, DEC_BATCH, CONV_W - 1, CONV_CH)),
        'meta_tokens': nrm(k[10], (N_META, D_MODEL)),
        'w_norm_mix': gain(k[11], D_MODEL),
        'w_in': nrm(k[12], (DEPTH, D_MODEL, in_cols), D_MODEL ** -0.5),
        'conv_w': nrm(k[13], (DEPTH, CONV_W, CONV_CH), CONV_W ** -0.5),
        'conv_b': nrm(k[14], (DEPTH, CONV_CH), 0.02),
        'dt_bias': dt_bias,
        'a_log': a_log,
        'd_skip': 1.0 + 0.1 * jax.random.normal(k[15], (DEPTH, SSM_HEADS), f32),
        'w_ssm_norm': gain(k[16], D_INNER),
        'w_ssm_out': nrm(k[17], (DEPTH, D_INNER, D_MODEL), D_INNER ** -0.5),
        'w_kv_norm': gain(k[18], KV_LORA),
        'q_norm': gain(k[19], QK_HEAD),
        'k_norm': gain(k[20], QK_HEAD),
        'w_kv_up': nrm(k[21], (DEPTH, KV_LORA, MLA_HEADS, QK_NOPE + V_DIM), KV_LORA ** -0.5),
        'w_attn_out': nrm(k[22], (DEPTH, MLA_HEADS * V_DIM, D_MODEL), (MLA_HEADS * V_DIM) ** -0.5),
        'w_out': nrm(k[23], (DEPTH, D_MODEL, D_MODEL), D_MODEL ** -0.5),
        'w_norm_ffn': gain(k[24], D_MODEL),
        'w_ffn_in': nrm(k[25], (DEPTH, D_MODEL, 2 * FFN_HIDDEN), D_MODEL ** -0.5),
        'w_ffn_out': nrm(k[26], (DEPTH, FFN_HIDDEN, D_MODEL), FFN_HIDDEN ** -0.5),
    }


def reference(x_prompt, x_sample, cache_ckv, cache_kpe, cache_kscale, page_table, state_ssm, state_conv,
              meta_tokens, w_norm_mix, w_in, conv_w, conv_b, dt_bias, a_log, d_skip, w_ssm_norm, w_ssm_out,
              w_kv_norm, q_norm, k_norm, w_kv_up, w_attn_out, w_out, w_norm_ffn, w_ffn_in, w_ffn_out):
    bp = x_prompt.shape[0]
    ts = x_sample.shape[1]
    meta = jnp.broadcast_to(meta_tokens.astype(x_prompt.dtype)[None], (bp, N_META, D_MODEL))
    h_p = jnp.concatenate([meta, x_prompt], axis=1)
    h_s = x_sample
    pos_p = jnp.arange(h_p.shape[1])
    pos_s = PAST_LEN + jnp.arange(ts)
    ckv_p, kpe_p, ksc_p, ssm_p, conv_p = [], [], [], [], []
    ckv_s, kpe_s, ksc_s, ssm_s, conv_s = [], [], [], [], []
    for i in range(DEPTH):
        z, xbc, dtr, q, ckv_raw, kr, ga, gs = _project(h_p, w_norm_mix[i], w_in[i])
        conv0 = jnp.zeros((bp, CONV_W - 1, CONV_CH), xbc.dtype)
        xs, dt, da, bm, cm, conv_new = _ssm_pre(xbc, dtr, conv0, conv_w[i], conv_b[i], dt_bias[i], a_log[i])
        y, h_last = _ssd_prompt(xs, dt, da, bm, cm)
        ssm_out = _ssm_post(y, xs, z, d_skip[i], w_ssm_norm[i], w_ssm_out[i])
        qn, qp, ckv, kn, kpe, ksc = _mla_qk(q, ckv_raw, kr, pos_p, w_kv_norm[i], q_norm[i], k_norm[i], w_kv_up[i])
        att = _mla_prompt(qn, qp, ckv, kn, kpe, ksc, k_norm[i], w_kv_up[i])
        h_p = _merge_ffn(h_p, att, ssm_out, ga, gs, w_attn_out[i], w_out[i], w_norm_ffn[i], w_ffn_in[i], w_ffn_out[i])
        ckv_p.append(ckv)
        kpe_p.append(kpe)
        ksc_p.append(ksc)
        ssm_p.append(h_last)
        conv_p.append(conv_new)
        z, xbc, dtr, q, ckv_raw, kr, ga, gs = _project(h_s, w_norm_mix[i], w_in[i])
        xs, dt, da, bm, cm, conv_new = _ssm_pre(xbc, dtr, state_conv[i], conv_w[i], conv_b[i], dt_bias[i], a_log[i])
        y, h_last = _ssd_block(state_ssm[i].astype(jnp.float32), xs, dt, da, bm, cm)
        ssm_out = _ssm_post(y, xs, z, d_skip[i], w_ssm_norm[i], w_ssm_out[i])
        qn, qp, ckv, _, kpe, ksc = _mla_qk(q, ckv_raw, kr, pos_s, w_kv_norm[i], q_norm[i], k_norm[i], w_kv_up[i])
        att = _mla_sample(qn, qp, ckv, kpe, ksc, cache_ckv, cache_kpe, cache_kscale, i, page_table, k_norm[i], w_kv_up[i])
        h_s = _merge_ffn(h_s, att, ssm_out, ga, gs, w_attn_out[i], w_out[i], w_norm_ffn[i], w_ffn_in[i], w_ffn_out[i])
        ckv_s.append(ckv)
        kpe_s.append(kpe)
        ksc_s.append(ksc)
        ssm_s.append(h_last)
        conv_s.append(conv_new)
    y_prompt = h_p[:, N_META:]
    y_sample = h_s
    return (y_prompt, y_sample,
            jnp.stack(ckv_p), jnp.stack(kpe_p), jnp.stack(ksc_p), jnp.stack(ssm_p), jnp.stack(conv_p),
            jnp.stack(ckv_s), jnp.stack(kpe_s), jnp.stack(ksc_s), jnp.stack(ssm_s), jnp.stack(conv_s))
```

```python
import functools

import jax
import jax.numpy as jnp
from jax import lax
from jax.experimental import pallas as pl
from jax.experimental.pallas import tpu as pltpu

EPS = 1e-6
ROPE_THETA = 10000.0
CHUNK = 128
VMEM_LIMIT_BYTES = 56 * 1024 * 1024
LANES = 128
SUBLANES = 8

F32 = jnp.float32
BF16 = jnp.bfloat16


def _cp(*sem):
    return pltpu.CompilerParams(dimension_semantics=sem, vmem_limit_bytes=VMEM_LIMIT_BYTES)


def _pick(n, pref, align=SUBLANES):
    if n <= pref:
        return n
    for t in range(pref - pref % align, 0, -align):
        if n % t == 0:
            return t
    return n


def _sigmoid(x):
    return 1.0 / (1.0 + jnp.exp(-x))


def _silu(x):
    return x * _sigmoid(x)


def _dot(a, b):
    return jnp.dot(a, b, preferred_element_type=F32)


def _dot_nt(a, b):
    return lax.dot_general(a, b, (((1,), (1,)), ((), ())), preferred_element_type=F32)


def _dot_tn(a, b):
    return lax.dot_general(a, b, (((0,), (0,)), ((), ())), preferred_element_type=F32)


def _split3(x):
    hi = x.astype(BF16)
    r = x - hi.astype(F32)
    mid = r.astype(BF16)
    lo = (r - mid.astype(F32)).astype(BF16)
    return hi, mid, lo


def _rope(x, cos2, sin2):
    half = x.shape[-1] // 2
    sw = jnp.concatenate([x[:, half:], x[:, :half]], axis=-1)
    return x * cos2 + sw * sin2


def _rmsnorm_body(x_ref, w_ref, o_ref):
    x = x_ref[...].astype(F32)
    ms = jnp.mean(x * x, axis=-1, keepdims=True)
    o_ref[...] = (x * lax.rsqrt(ms + EPS) * w_ref[...]).astype(o_ref.dtype)


def _rmsnorm(x, w):
    m, d = x.shape
    tm = _pick(m, 512)
    return pl.pallas_call(
        _rmsnorm_body, grid=(m // tm,),
        in_specs=[pl.BlockSpec((tm, d), lambda i: (i, 0)), pl.BlockSpec((1, d), lambda i: (0, 0))],
        out_specs=pl.BlockSpec((tm, d), lambda i: (i, 0)),
        out_shape=jax.ShapeDtypeStruct((m, d), BF16),
        compiler_params=_cp("parallel"), name="rmsnorm")(x, w.reshape(1, d))


def _mm_body(x_ref, w_ref, o_ref):
    o_ref[...] = _dot(x_ref[...], w_ref[...]).astype(o_ref.dtype)


def _matmul(x, w, out_dtype, tm_pref=1024, tn_pref=1024, name="matmul"):
    m, k = x.shape
    n = w.shape[1]
    tm = _pick(m, tm_pref)
    tn = _pick(n, tn_pref, LANES)
    return pl.pallas_call(
        _mm_body, grid=(m // tm, n // tn),
        in_specs=[pl.BlockSpec((tm, k), lambda i, j: (i, 0)), pl.BlockSpec((k, tn), lambda i, j: (0, j))],
        out_specs=pl.BlockSpec((tm, tn), lambda i, j: (i, j)),
        out_shape=jax.ShapeDtypeStruct((m, n), out_dtype),
        compiler_params=_cp("parallel", "arbitrary"), name=name)(x, w)


def _prep_q_body(qn_ref, qr_ref, cos_ref, sin_ref, wn_ref, wr_ref, o_ref, *, hm, nope, rope, scale):
    cos2 = cos_ref[...]
    sin2 = sin_ref[...]
    wn = wn_ref[...]
    wr = wr_ref[...]
    for h in range(hm):
        n = qn_ref[:, h * nope:(h + 1) * nope].astype(F32)
        r = qr_ref[:, h * rope:(h + 1) * rope].astype(F32)
        ms = (jnp.sum(n * n, axis=-1, keepdims=True) + jnp.sum(r * r, axis=-1, keepdims=True)) / (nope + rope)
        inv = lax.rsqrt(ms + EPS)
        rr = _rope(r * inv * wr, cos2, sin2)
        o_ref[h] = (jnp.concatenate([n * inv * wn, rr], axis=-1) * scale).astype(o_ref.dtype)


def _prep_q(proj, off_qn, off_qr, cos2, sin2, wn, wr, *, hm, nope, rope, scale, seq):
    rows = proj.shape[0]
    tm = _pick(seq, 512)
    nseq = seq // tm
    body = functools.partial(_prep_q_body, hm=hm, nope=nope, rope=rope, scale=scale)
    return pl.pallas_call(
        body, grid=(rows // tm,),
        in_specs=[pl.BlockSpec((tm, hm * nope), lambda i: (i, off_qn // (hm * nope))),
                  pl.BlockSpec((tm, hm * rope), lambda i: (i, off_qr // (hm * rope))),
                  pl.BlockSpec((tm, rope), lambda i: (i % nseq, 0)),
                  pl.BlockSpec((tm, rope), lambda i: (i % nseq, 0)),
                  pl.BlockSpec((1, nope), lambda i: (0, 0)),
                  pl.BlockSpec((1, rope), lambda i: (0, 0))],
        out_specs=pl.BlockSpec((hm, tm, nope + rope), lambda i: (0, i, 0)),
        out_shape=jax.ShapeDtypeStruct((hm, rows, nope + rope), BF16),
        compiler_params=_cp("parallel"), name="mla_prep_q")(proj, proj, cos2, sin2, wn, wr)


def _prep_kv_body(ckv_ref, dk_ref, cos_ref, sin_ref, wc_ref, wk_ref, wv_ref, kn_ref, kr_ref,
                  ckv_o, kpe_o, ks_o, k_o, v_o, *, hm, nope, rope, vd, hs):
    x = ckv_ref[...].astype(F32)
    ms = jnp.mean(x * x, axis=-1, keepdims=True)
    ckv = x * lax.rsqrt(ms + EPS) * wc_ref[...]
    ckv_o[...] = ckv
    cb = ckv.astype(BF16)
    kn = _dot(cb, wk_ref[...])
    v = _dot(cb, wv_ref[...])
    kr = dk_ref[:, hs:hs + rope].astype(F32)
    krss = jnp.sum(kr * kr, axis=-1, keepdims=True)
    kpe = _rope(kr * kr_ref[...], cos_ref[...], sin_ref[...])
    kpe_o[...] = kpe
    tm = x.shape[0]
    lane = lax.broadcasted_iota(jnp.int32, (tm, hm), 1)
    ks_all = jnp.zeros((tm, hm), F32)
    for h in range(hm):
        knh = kn[:, h * nope:(h + 1) * nope]
        ss = jnp.sum(knh * knh, axis=-1, keepdims=True) + krss
        ksh = lax.rsqrt(ss / (nope + rope) + EPS)
        ks_all = jnp.where(lane == h, ksh, ks_all)
        k_o[h] = (jnp.concatenate([knh * kn_ref[...], kpe], axis=-1) * ksh).astype(k_o.dtype)
        v_o[h] = v[:, h * vd:(h + 1) * vd].astype(v_o.dtype)
    ks_o[...] = ks_all


def _prep_kv(tail, cos2, sin2, wc, wk, wv, kn_w, kr_w, *, hm, nope, rope, vd, lora, hs, seq):
    rows = tail.shape[0]
    tm = _pick(seq, 512)
    nseq = seq // tm
    body = functools.partial(_prep_kv_body, hm=hm, nope=nope, rope=rope, vd=vd, hs=hs)
    full = lambda i: (0, 0)
    return pl.pallas_call(
        body, grid=(rows // tm,),
        in_specs=[pl.BlockSpec((tm, lora), lambda i: (i, 0)),
                  pl.BlockSpec((tm, hs + rope), lambda i: (i, lora // (hs + rope))),
                  pl.BlockSpec((tm, rope), lambda i: (i % nseq, 0)),
                  pl.BlockSpec((tm, rope), lambda i: (i % nseq, 0)),
                  pl.BlockSpec((1, lora), full),
                  pl.BlockSpec((lora, hm * nope), full),
                  pl.BlockSpec((lora, hm * vd), full),
                  pl.BlockSpec((1, nope), full),
                  pl.BlockSpec((1, rope), full)],
        out_specs=[pl.BlockSpec((tm, lora), lambda i: (i, 0)),
                   pl.BlockSpec((tm, rope), lambda i: (i, 0)),
                   pl.BlockSpec((tm, hm), lambda i: (i, 0)),
                   pl.BlockSpec((hm, tm, nope + rope), lambda i: (0, i, 0)),
                   pl.BlockSpec((hm, tm, vd), lambda i: (0, i, 0))],
        out_shape=[jax.ShapeDtypeStruct((rows, lora), F32),
                   jax.ShapeDtypeStruct((rows, rope), F32),
                   jax.ShapeDtypeStruct((rows, hm), F32),
                   jax.ShapeDtypeStruct((hm, rows, nope + rope), BF16),
                   jax.ShapeDtypeStruct((hm, rows, vd), BF16)],
        compiler_params=_cp("parallel"), name="mla_prep_kv")(tail, tail, cos2, sin2, wc, wk, wv, kn_w, kr_w)


def _flash_body(q_ref, kx_ref, vx_ref, km_ref, vm_ref, o_ref, *, tq):
    i = pl.program_id(2)
    q = q_ref[0]
    sm = _dot_nt(q, km_ref[0])
    m0 = jnp.max(sm, axis=-1, keepdims=True)
    p0 = jnp.exp(sm - m0)
    l0 = jnp.sum(p0, axis=-1, keepdims=True)
    acc0 = _dot(p0.astype(BF16), vm_ref[0])

    def step(j, carry, diagonal):
        m, l, acc = carry
        start = pl.multiple_of(j * tq, tq)
        k = kx_ref[0, 0, pl.ds(start, tq), :]
        v = vx_ref[0, 0, pl.ds(start, tq), :]
        s = _dot_nt(q, k)
        if diagonal:
            row = lax.broadcasted_iota(jnp.int32, (tq, tq), 0)
            col = lax.broadcasted_iota(jnp.int32, (tq, tq), 1)
            s = jnp.where(col <= row, s, -jnp.inf)
        m_new = jnp.maximum(m, jnp.max(s, axis=-1, keepdims=True))
        corr = jnp.exp(m - m_new)
        p = jnp.exp(s - m_new)
        l = l * corr + jnp.sum(p, axis=-1, keepdims=True)
        acc = acc * corr + _dot(p.astype(BF16), v)
        return m_new, l, acc

    carry = lax.fori_loop(0, i, lambda j, c: step(j, c, False), (m0, l0, acc0))
    _, l, acc = step(i, carry, True)
    o_ref[...] = (acc / l).astype(o_ref.dtype)


def _flash(q, kx, vx, km, vm, *, batch, seq):
    hm, rows, dk = q.shape
    vd = vx.shape[-1]
    nm = km.shape[1]
    tq = _pick(seq, 512)
    nq = seq // tq
    body = functools.partial(_flash_body, tq=tq)
    return pl.pallas_call(
        body, grid=(batch, hm, nq),
        in_specs=[pl.BlockSpec((1, tq, dk), lambda b, h, i: (h, b * nq + i, 0)),
                  pl.BlockSpec((1, 1, seq, dk), lambda b, h, i: (h, b, 0, 0)),
                  pl.BlockSpec((1, 1, seq, vd), lambda b, h, i: (h, b, 0, 0)),
                  pl.BlockSpec((1, nm, dk), lambda b, h, i: (h, 0, 0)),
                  pl.BlockSpec((1, nm, vd), lambda b, h, i: (h, 0, 0))],
        out_specs=pl.BlockSpec((tq, vd), lambda b, h, i: (b * nq + i, h)),
        out_shape=jax.ShapeDtypeStruct((rows, hm * vd), BF16),
        compiler_params=_cp("parallel", "parallel", "arbitrary"), name="mla_flash")(q, kx, vx, km, vm)


def _ssd_body(z_ref, xs_ref, b_ref, c_ref, dk_ref, cwx_ref, cbx_ref, cwb_ref, cbb_ref, cwc_ref, cbc_ref,
              dtb_ref, alog_ref, dsk_ref, wn_ref, h0_ref, hx_ref, hb_ref, hc_ref,
              yn_ref, st_ref, ht_ref,
              hs, bufx, bufb, bufc, cs_s, cst_s, dtt_s, *, L, P, N, gs, e, lead, nch, ng):
    c = pl.program_id(1)
    gi = pl.program_id(2)
    nh = gs * e

    @pl.when(c == 0)
    def _():
        hs[gi] = h0_ref[...]
        bufx[gi, 0:SUBLANES, :] = hx_ref[...]
        bufb[gi, 0:SUBLANES, :] = hb_ref[...]
        bufc[gi, 0:SUBLANES, :] = hc_ref[...]

    rows = lax.broadcasted_iota(jnp.int32, (L, 1), 0)
    live = rows >= lead

    @pl.when(gi == 0)
    def _():
        x = dk_ref[...].astype(F32) + dtb_ref[...]
        dt = jnp.maximum(x, 0.0) + jnp.log(1.0 + jnp.exp(-jnp.abs(x)))
        if lead:
            dt = jnp.where(live, dt, 0.0)
        da = dt * (-jnp.exp(alog_ref[...]))
        r = lax.broadcasted_iota(jnp.int32, (L, L), 0)
        s = lax.broadcasted_iota(jnp.int32, (L, L), 1)
        tri = (r >= s).astype(BF16)
        hi, mid, lo = _split3(da)
        cs = _dot(tri, hi) + _dot(tri, mid) + _dot(tri, lo)
        cst_s[...] = cs.T
        dtt_s[...] = dt.T
        for k in range(ng):
            cs_s[k] = pltpu.roll(cs, (LANES - nh * k) % LANES, axis=1) if k else cs

    def conv(buf, raw_ref, cw_ref, cb_ref):
        raw = raw_ref[...].astype(F32)
        buf[gi, SUBLANES:SUBLANES + L, :] = raw
        out = cb_ref[...] + cw_ref[3:4, :] * raw
        out = out + cw_ref[2:3, :] * buf[gi, SUBLANES - 1:SUBLANES - 1 + L, :]
        out = out + cw_ref[1:2, :] * buf[gi, SUBLANES - 2:SUBLANES - 2 + L, :]
        out = out + cw_ref[0:1, :] * buf[gi, SUBLANES - 3:SUBLANES - 3 + L, :]
        buf[gi, 0:SUBLANES, :] = buf[gi, L:L + SUBLANES, :]
        out = _silu(out)
        if lead:
            out = jnp.where(live, out, 0.0)
        return out

    xs = conv(bufx, xs_ref, cwx_ref, cbx_ref)
    bm = conv(bufb, b_ref, cwb_ref, cbb_ref)
    cm = conv(bufc, c_ref, cwc_ref, cbc_ref)

    cs = cs_s[gi]
    h_first = pl.multiple_of(gi * nh, SUBLANES)
    cst = cst_s[pl.ds(h_first, nh), :]
    dtt = dtt_s[pl.ds(h_first, nh), :]
    r = lax.broadcasted_iota(jnp.int32, (L, L), 0)
    s = lax.broadcasted_iota(jnp.int32, (L, L), 1)
    causal = r >= s
    lo_half = lax.broadcasted_iota(jnp.int32, (1, LANES), 1) < P
    hold_all = hs[gi]

    def head(j, cb, cg):
        col = cs[:, j:j + 1]
        dec = jnp.exp(jnp.where(causal, col - cst[j:j + 1, :], -jnp.inf))
        ecol = jnp.exp(col)
        mh = (cb * dec * dtt[j:j + 1, :]).astype(BF16)
        ch = (cg * ecol).astype(BF16)
        w_end = dec[L - 1:L, :] * dtt[j:j + 1, :]
        return mh, ch, w_end, ecol[L - 1:L, :]

    ys, hnew = [], []
    for gl in range(gs):
        bg = bm[:, gl * N:(gl + 1) * N]
        cg = cm[:, gl * N:(gl + 1) * N]
        cb = _dot_nt(cg.astype(BF16), bg.astype(BF16))
        bt = bg.T
        for pr in range(e // 2):
            j0 = gl * e + 2 * pr
            lane0 = (gl * (e // 2) + pr) * LANES
            m0, c0, w0, d0 = head(j0, cb, cg)
            m1, c1, w1, d1 = head(j0 + 1, cb, cg)
            xp = xs[:, lane0:lane0 + LANES]
            hold = hold_all[:, lane0:lane0 + LANES]
            x_lo = jnp.where(lo_half, xp, 0.0).astype(BF16)
            x_hi = jnp.where(lo_half, 0.0, xp).astype(BF16)
            h_lo = jnp.where(lo_half, hold, 0.0).astype(BF16)
            h_hi = jnp.where(lo_half, 0.0, hold).astype(BF16)
            ys.append(_dot(jnp.concatenate([m0, c0, m1, c1], axis=1),
                           jnp.concatenate([x_lo, h_lo, x_hi, h_hi], axis=0)))
            a = jnp.concatenate([(bt * w0).astype(BF16), (bt * w1).astype(BF16)], axis=1)
            upd = _dot(a, jnp.concatenate([x_lo, x_hi], axis=0))
            hnew.append(hold * jnp.where(lo_half, d0, d1) + upd)
    h_all = jnp.concatenate(hnew, axis=1)
    hs[gi] = h_all

    y = jnp.concatenate(ys, axis=1)
    y = (y + dsk_ref[...] * xs) * _silu(z_ref[...].astype(F32))
    gw = e * P
    outs = []
    for gl in range(gs):
        yg = y[:, gl * gw:(gl + 1) * gw]
        ms = jnp.mean(yg * yg, axis=-1, keepdims=True)
        outs.append(yg * lax.rsqrt(ms + EPS))
    yn_ref[...] = (jnp.concatenate(outs, axis=1) * wn_ref[...]).astype(yn_ref.dtype)

    @pl.when(c == nch - 1)
    def _():
        st_ref[0] = h_all.T
        ht_ref[0] = h_all


def _ssd(proj, tail, off_z, off_xs, off_b, off_c, off_dt, conv_wx, conv_bx, conv_wb, conv_bb, conv_wc, conv_bc,
         dtb, alog, dsk, wn, h0t, halo_x, halo_b, halo_c, *, batch, nch, P, N, G, H, lead=0, gs=2):
    L = CHUNK
    e = H // G
    di = H * P
    gw = gs * e * P
    gn = gs * N
    ng = G // gs
    body = functools.partial(_ssd_body, L=L, P=P, N=N, gs=gs, e=e, lead=lead, nch=nch, ng=ng)
    row = lambda b, c, g: b * nch + c
    last = lambda c, g: jnp.where(c == nch - 1, g, 0)
    col0 = lambda b, c, g: (0, g)
    fixed = lambda b, c, g: (0, 0)
    in_specs = [
        pl.BlockSpec((L, gw), lambda b, c, g: (row(b, c, g), off_z // gw + g)),
        pl.BlockSpec((L, gw), lambda b, c, g: (row(b, c, g), off_xs // gw + g)),
        pl.BlockSpec((L, gn), lambda b, c, g: (row(b, c, g), off_b // gn + g)),
        pl.BlockSpec((L, gn), lambda b, c, g: (row(b, c, g), off_c // gn + g)),
        pl.BlockSpec((L, LANES), lambda b, c, g: (row(b, c, g), off_dt // LANES)),
        pl.BlockSpec((4, gw), col0), pl.BlockSpec((1, gw), col0),
        pl.BlockSpec((4, gn), col0), pl.BlockSpec((1, gn), col0),
        pl.BlockSpec((4, gn), col0), pl.BlockSpec((1, gn), col0),
        pl.BlockSpec((1, LANES), fixed), pl.BlockSpec((1, LANES), fixed),
        pl.BlockSpec((1, gw), col0), pl.BlockSpec((1, gw), col0),
        pl.BlockSpec((N, gw), col0),
        pl.BlockSpec((SUBLANES, gw), col0), pl.BlockSpec((SUBLANES, gn), col0), pl.BlockSpec((SUBLANES, gn), col0),
    ]
    out_specs = [
        pl.BlockSpec((L, gw), lambda b, c, g: (row(b, c, g), g)),
        pl.BlockSpec((1, gw, N), lambda b, c, g: (b, last(c, g), 0)),
        pl.BlockSpec((1, N, gw), lambda b, c, g: (b, 0, last(c, g))),
    ]
    out_shape = [jax.ShapeDtypeStruct((batch * nch * L, di), BF16),
                 jax.ShapeDtypeStruct((batch, di, N), F32),
                 jax.ShapeDtypeStruct((batch, N, di), F32)]
    scratch = [pltpu.VMEM((ng, N, gw), F32),
               pltpu.VMEM((ng, L + SUBLANES, gw), F32),
               pltpu.VMEM((ng, L + SUBLANES, gn), F32),
               pltpu.VMEM((ng, L + SUBLANES, gn), F32),
               pltpu.VMEM((ng, L, LANES), F32),
               pltpu.VMEM((LANES, L), F32),
               pltpu.VMEM((LANES, L), F32)]
    return pl.pallas_call(
        body, grid=(batch, nch, ng), in_specs=in_specs, out_specs=out_specs, out_shape=out_shape,
        scratch_shapes=scratch, compiler_params=_cp("arbitrary", "arbitrary", "arbitrary"), name="ssd_scan")(
            proj, proj, proj, proj, tail, conv_wx, conv_bx, conv_wb, conv_bb, conv_wc, conv_bc,
            dtb, alog, dsk, wn, h0t, halo_x, halo_b, halo_c)


def _merge_body(att_ref, yn_ref, wa_ref, ws_ref, ga_ref, gs_ref, o_ref):
    a = _dot(att_ref[...], wa_ref[...])
    s = _dot(yn_ref[...], ws_ref[...])
    ga = _sigmoid(ga_ref[...].astype(F32))
    gs = _sigmoid(gs_ref[...].astype(F32))
    o_ref[...] = (ga * a + gs * s).astype(o_ref.dtype)


def _merge(att, yn, wa, ws, proj, off_ga, off_gs):
    m, ka = att.shape
    ks = yn.shape[1]
    n = wa.shape[1]
    tm = _pick(m, 1024)
    tn = _pick(n, 512, LANES)
    return pl.pallas_call(
        _merge_body, grid=(m // tm, n // tn),
        in_specs=[pl.BlockSpec((tm, ka), lambda i, j: (i, 0)),
                  pl.BlockSpec((tm, ks), lambda i, j: (i, 0)),
                  pl.BlockSpec((ka, tn), lambda i, j: (0, j)),
                  pl.BlockSpec((ks, tn), lambda i, j: (0, j)),
                  pl.BlockSpec((tm, tn), lambda i, j: (i, off_ga // tn + j)),
                  pl.BlockSpec((tm, tn), lambda i, j: (i, off_gs // tn + j))],
        out_specs=pl.BlockSpec((tm, tn), lambda i, j: (i, j)),
        out_shape=jax.ShapeDtypeStruct((m, n), BF16),
        compiler_params=_cp("parallel", "arbitrary"), name="gated_merge")(att, yn, wa, ws, proj, proj)


def _outproj_body(u_ref, wo_ref, h_ref, wn_ref, h2_ref, hn_ref):
    h2 = h_ref[...].astype(F32) + _dot(u_ref[...], wo_ref[...])
    h2_ref[...] = h2
    ms = jnp.mean(h2 * h2, axis=-1, keepdims=True)
    hn_ref[...] = (h2 * lax.rsqrt(ms + EPS) * wn_ref[...]).astype(hn_ref.dtype)


def _outproj(u, wo, h, wn):
    m, d = h.shape
    tm = _pick(m, 512)
    return pl.pallas_call(
        _outproj_body, grid=(m // tm,),
        in_specs=[pl.BlockSpec((tm, d), lambda i: (i, 0)),
                  pl.BlockSpec((d, d), lambda i: (0, 0)),
                  pl.BlockSpec((tm, d), lambda i: (i, 0)),
                  pl.BlockSpec((1, d), lambda i: (0, 0))],
        out_specs=[pl.BlockSpec((tm, d), lambda i: (i, 0)), pl.BlockSpec((tm, d), lambda i: (i, 0))],
        out_shape=[jax.ShapeDtypeStruct((m, d), F32), jax.ShapeDtypeStruct((m, d), BF16)],
        compiler_params=_cp("parallel"), name="out_proj")(u, wo, h, wn.reshape(1, d))


def _ffn_in_body(x_ref, wg_ref, wu_ref, o_ref):
    x = x_ref[...]
    o_ref[...] = (_silu(_dot(x, wg_ref[...])) * _dot(x, wu_ref[...])).astype(o_ref.dtype)


def _ffn_in(x, w, f):
    m, d = x.shape
    tm = _pick(m, 2048)
    tn = _pick(f, 512, LANES)
    nf = f // tn
    return pl.pallas_call(
        _ffn_in_body, grid=(m // tm, nf),
        in_specs=[pl.BlockSpec((tm, d), lambda i, j: (i, 0)),
                  pl.BlockSpec((d, tn), lambda i, j: (0, j)),
                  pl.BlockSpec((d, tn), lambda i, j: (0, nf + j))],
        out_specs=pl.BlockSpec((tm, tn), lambda i, j: (i, j)),
        out_shape=jax.ShapeDtypeStruct((m, f), BF16),
        compiler_params=_cp("parallel", "arbitrary"), name="ffn_in")(x, w, w)


def _ffn_out_body(a_ref, wd_ref, h_ref, o_ref):
    o_ref[...] = h_ref[...] + _dot(a_ref[...], wd_ref[...])


def _ffn_out(a, wd, h):
    m, f = a.shape
    d = wd.shape[1]
    tm = _pick(m, 1024)
    tn = _pick(d, 512, LANES)
    return pl.pallas_call(
        _ffn_out_body, grid=(m // tm, d // tn),
        in_specs=[pl.BlockSpec((tm, f), lambda i, j: (i, 0)),
                  pl.BlockSpec((f, tn), lambda i, j: (0, j)),
                  pl.BlockSpec((tm, tn), lambda i, j: (i, j))],
        out_specs=pl.BlockSpec((tm, tn), lambda i, j: (i, j)),
        out_shape=jax.ShapeDtypeStruct((m, d), F32),
        compiler_params=_cp("parallel", "arbitrary"), name="ffn_out")(a, wd, h)


def _spre_body(x_ref, s0_ref, s1_ref, s2_ref, cw_ref, cb_ref, xc_ref, xt_ref):
    x = x_ref[...].astype(F32)
    out = cb_ref[...] + cw_ref[0:1, :] * s0_ref[...] + cw_ref[1:2, :] * s1_ref[...] \
        + cw_ref[2:3, :] * s2_ref[...] + cw_ref[3:4, :] * x
    xc = _silu(out)
    xc_ref[...] = xc
    xt_ref[...] = xc.T.astype(xt_ref.dtype)


def _spre(proj, off_xbc, sconv, conv_w, conv_b, *, db, cc):
    tn = _pick(cc, 2048, LANES)
    nb = cc // tn
    return pl.pallas_call(
        _spre_body, grid=(nb,),
        in_specs=[pl.BlockSpec((db, tn), lambda j: (0, off_xbc // tn + j)),
                  pl.BlockSpec((db, tn), lambda j: (0, j)),
                  pl.BlockSpec((db, tn), lambda j: (0, nb + j)),
                  pl.BlockSpec((db, tn), lambda j: (0, 2 * nb + j)),
                  pl.BlockSpec((4, tn), lambda j: (0, j)),
                  pl.BlockSpec((1, tn), lambda j: (0, j))],
        out_specs=[pl.BlockSpec((db, tn), lambda j: (0, j)), pl.BlockSpec((tn, db), lambda j: (j, 0))],
        out_shape=[jax.ShapeDtypeStruct((db, cc), F32), jax.ShapeDtypeStruct((cc, db), BF16)],
        compiler_params=_cp("parallel"), name="sample_conv")(proj, sconv, sconv, sconv, conv_w, conv_b)


def _sdt_body(dk_ref, dtb_ref, alog_ref, dt_ref, dec_ref):
    x = dk_ref[...].astype(F32) + dtb_ref[...]
    dt = jnp.maximum(x, 0.0) + jnp.log(1.0 + jnp.exp(-jnp.abs(x)))
    dt_ref[...] = dt
    dec_ref[...] = jnp.exp(dt * (-jnp.exp(alog_ref[...])))


def _sdt(tail, off_dt, dtb, alog, *, db):
    blk = pl.BlockSpec((db, LANES), lambda i: (0, 0))
    return pl.pallas_call(
        _sdt_body, grid=(1,),
        in_specs=[pl.BlockSpec((db, LANES), lambda i: (0, off_dt // LANES)),
                  pl.BlockSpec((1, LANES), lambda i: (0, 0)), pl.BlockSpec((1, LANES), lambda i: (0, 0))],
        out_specs=[blk, blk],
        out_shape=[jax.ShapeDtypeStruct((db, LANES), F32), jax.ShapeDtypeStruct((db, LANES), F32)],
        compiler_params=_cp("arbitrary"), name="sample_dt")(tail, dtb, alog)


def _sstate_body(dec_ref, dts_ref, st_ref, xt_ref, b_ref, c_ref, sto_ref, yo_ref, *, H, P, N, G, db):
    b = pl.program_id(0)
    e = H // G
    di = H * P
    bb = b_ref[0]
    h0 = st_ref[0].reshape(di, N)
    y = _dot_nt(c_ref[0].astype(BF16), h0.astype(BF16))
    grow = lax.broadcasted_iota(jnp.int32, (G, di), 0)
    gcol = lax.broadcasted_iota(jnp.int32, (G, di), 1) // (e * P)
    yo_ref[0] = jnp.sum(jnp.where(grow == gcol, y, 0.0), axis=0, keepdims=True)
    onehot = lax.broadcasted_iota(jnp.int32, (db, N), 0) == b
    for g in range(G):
        mg = jnp.where(onehot, bb[g:g + 1, :], 0.0).astype(BF16)
        u = _dot(xt_ref[g * e * P:(g + 1) * e * P, :], mg)
        for hh in range(e):
            h = g * e + hh
            sto_ref[0, h] = st_ref[0, h] * dec_ref[b, h] + dts_ref[b, h] * u[hh * P:(hh + 1) * P, :]


def _sstate(dec, dts, state, xt, bm, cm, *, H, P, N, G):
    db = state.shape[0]
    di = H * P
    body = functools.partial(_sstate_body, H=H, P=P, N=N, G=G, db=db)
    smem = pl.BlockSpec(memory_space=pltpu.SMEM)
    return pl.pallas_call(
        body, grid=(db,),
        in_specs=[smem, smem,
                  pl.BlockSpec((1, H, P, N), lambda b: (b, 0, 0, 0)),
                  pl.BlockSpec((di, db), lambda b: (0, 0)),
                  pl.BlockSpec((1, G, N), lambda b: (b, 0, 0)),
                  pl.BlockSpec((1, G, N), lambda b: (b, 0, 0))],
        out_specs=[pl.BlockSpec((1, H, P, N), lambda b: (b, 0, 0, 0)),
                   pl.BlockSpec((1, 1, di), lambda b: (b, 0, 0))],
        out_shape=[jax.ShapeDtypeStruct((db, H, P, N), F32), jax.ShapeDtypeStruct((db, 1, di), F32)],
        compiler_params=_cp("parallel"), name="sample_state")(dec, dts, state, xt, bm, cm)


def _spost_body(xs_ref, b_ref, c_ref, yo_ref, dt_ref, dec_ref, z_ref, dsk_ref, wn_ref, o_ref, *, H, P, N, G):
    e = H // G
    di = H * P
    xs = xs_ref[...]
    hrow = lax.broadcasted_iota(jnp.int32, (LANES, di), 0)
    hcol = lax.broadcasted_iota(jnp.int32, (LANES, di), 1) // P
    expand = (hrow == hcol).astype(BF16)

    def rep(v):
        hi, mid, lo = _split3(v)
        return _dot(hi, expand) + _dot(mid, expand) + _dot(lo, expand)

    dt_rep = rep(dt_ref[...])
    dec_rep = rep(dec_ref[...])
    prod = b_ref[...] * c_ref[...]
    gw = e * P
    outs = []
    for g in range(G):
        cbg = jnp.sum(prod[:, g * N:(g + 1) * N], axis=-1, keepdims=True)
        sl = slice(g * gw, (g + 1) * gw)
        y = cbg * dt_rep[:, sl] * xs[:, sl] + dec_rep[:, sl] * yo_ref[:, sl] + dsk_ref[:, sl] * xs[:, sl]
        y = y * _silu(z_ref[:, sl].astype(F32))
        ms = jnp.mean(y * y, axis=-1, keepdims=True)
        outs.append(y * lax.rsqrt(ms + EPS))
    o_ref[...] = (jnp.concatenate(outs, axis=1) * wn_ref[...]).astype(o_ref.dtype)


def _spost(xc, yoff, dt, dec, proj, off_z, dsk, wn, *, H, P, N, G):
    db = xc.shape[0]
    di = H * P
    gn = G * N
    body = functools.partial(_spost_body, H=H, P=P, N=N, G=G)
    one = lambda i: (0, 0)
    return pl.pallas_call(
        body, grid=(1,),
        in_specs=[pl.BlockSpec((db, di), one),
                  pl.BlockSpec((db, gn), lambda i: (0, di // gn)),
                  pl.BlockSpec((db, gn), lambda i: (0, di // gn + 1)),
                  pl.BlockSpec((db, di), one),
                  pl.BlockSpec((db, LANES), one), pl.BlockSpec((db, LANES), one),
                  pl.BlockSpec((db, di), lambda i: (0, off_z // di)),
                  pl.BlockSpec((1, di), one), pl.BlockSpec((1, di), one)],
        out_specs=pl.BlockSpec((db, di), one),
        out_shape=jax.ShapeDtypeStruct((db, di), BF16),
        compiler_params=_cp("arbitrary"), name="sample_ssm_post")(xc, xc, xc, yoff, dt, dec, proj, dsk, wn)


def _sq_body(qn_ref, qr_ref, cos_ref, sin_ref, wn_ref, wr_ref, kn_ref, wk_ref, qa_ref, qo_ref, *, hm, nope, rope, scale):
    cos2 = cos_ref[...]
    sin2 = sin_ref[...]
    for h in range(hm):
        n = qn_ref[:, h * nope:(h + 1) * nope].astype(F32)
        r = qr_ref[:, h * rope:(h + 1) * rope].astype(F32)
        ms = (jnp.sum(n * n, axis=-1, keepdims=True) + jnp.sum(r * r, axis=-1, keepdims=True)) / (nope + rope)
        inv = lax.rsqrt(ms + EPS)
        rr = _rope(r * inv * wr_ref[...], cos2, sin2)
        qn = (n * inv * wn_ref[...] * kn_ref[...]).astype(BF16)
        qa_ref[h] = (_dot_nt(qn, wk_ref[h]) * scale).astype(qa_ref.dtype)
        qo_ref[h] = (rr * scale).astype(qo_ref.dtype)


def _sq(proj, off_qn, off_qr, cos2, sin2, wn, wr, kn_w, wk_h, *, hm, nope, rope, lora, scale, db):
    body = functools.partial(_sq_body, hm=hm, nope=nope, rope=rope, scale=scale)
    one = lambda i: (0, 0)
    return pl.pallas_call(
        body, grid=(1,),
        in_specs=[pl.BlockSpec((db, hm * nope), lambda i: (0, off_qn // (hm * nope))),
                  pl.BlockSpec((db, hm * rope), lambda i: (0, off_qr // (hm * rope))),
                  pl.BlockSpec((db, rope), one), pl.BlockSpec((db, rope), one),
                  pl.BlockSpec((1, nope), one), pl.BlockSpec((1, rope), one), pl.BlockSpec((1, nope), one),
                  pl.BlockSpec((hm, lora, nope), lambda i: (0, 0, 0))],
        out_specs=[pl.BlockSpec((hm, db, lora), lambda i: (0, 0, 0)), pl.BlockSpec((hm, db, rope), lambda i: (0, 0, 0))],
        out_shape=[jax.ShapeDtypeStruct((hm, db, lora), BF16), jax.ShapeDtypeStruct((hm, db, rope), BF16)],
        compiler_params=_cp("arbitrary"), name="sample_q_absorb")(proj, proj, cos2, sin2, wn, wr, kn_w, wk_h)


def _row_to_col(r, hm):
    ri = lax.broadcasted_iota(jnp.int32, (hm, hm), 0)
    ci = lax.broadcasted_iota(jnp.int32, (hm, hm), 1)
    d = jnp.where(ri == ci, r, 0.0)
    ones = jnp.ones((hm, LANES), BF16)
    hi, mid, lo = _split3(d)
    return _dot(hi, ones) + _dot(mid, ones) + _dot(lo, ones)


def _decode_body(pt_ref, qa_ref, qr_ref, cn_ref, kn_ref, sn_ref, *rest, G, PS, hm, lora):
    ck, kp, ks = rest[0:G], rest[G:2 * G], rest[2 * G:3 * G]
    o_ref = rest[3 * G]
    m_s, l_s, acc_s, cbuf, kbuf = rest[3 * G + 1:]
    j = pl.program_id(1)
    nj = pl.num_programs(1)
    reps = lora // LANES

    @pl.when(j == 0)
    def _():
        m_s[...] = jnp.full(m_s.shape, -jnp.inf, F32)
        l_s[...] = jnp.zeros(l_s.shape, F32)
        acc_s[...] = jnp.zeros(acc_s.shape, F32)

    for g in range(G):
        cbuf[g * PS:(g + 1) * PS, :] = ck[g][0, 0].astype(BF16)
        kbuf[g * PS:(g + 1) * PS, :] = kp[g][0, 0].astype(BF16)
    qa = qa_ref[0]
    qr = qr_ref[0]
    sc = _dot_nt(cbuf[...], qa) + _dot_nt(kbuf[...], qr)
    sc = sc * jnp.concatenate([ks[g][0, 0] for g in range(G)], axis=0)
    m_old = m_s[...]
    m_new = jnp.maximum(m_old, jnp.max(sc, axis=0, keepdims=True))
    corr = jnp.exp(m_old - m_new)
    p = jnp.exp(sc - m_new)
    l_new = l_s[...] * corr + jnp.sum(p, axis=0, keepdims=True)
    pv = _dot_tn(p.astype(BF16), cbuf[...])
    acc = acc_s[...] * jnp.tile(_row_to_col(corr, hm), (1, reps)) + pv
    m_s[...] = m_new
    l_s[...] = l_new
    acc_s[...] = acc

    @pl.when(j == nj - 1)
    def _():
        cn = cn_ref[0].astype(F32)
        cn8 = jnp.broadcast_to(cn, (SUBLANES, lora)).astype(BF16)
        kn8 = jnp.broadcast_to(kn_ref[0].astype(F32), (SUBLANES, kn_ref.shape[-1])).astype(BF16)
        scn = (_dot_nt(cn8, qa) + _dot_nt(kn8, qr))[0:1, :] * sn_ref[0]
        m_f = jnp.maximum(m_new, scn)
        corr2 = jnp.exp(m_new - m_f)
        pn = jnp.exp(scn - m_f)
        l_f = l_new * corr2 + pn
        acc_f = acc * jnp.tile(_row_to_col(corr2, hm), (1, reps)) + jnp.tile(_row_to_col(pn, hm), (1, reps)) * cn
        o_ref[0] = acc_f / jnp.tile(_row_to_col(l_f, hm), (1, reps))


def _decode(page_table, qa, qr, ckv_n, kpe_n, ks_n, cache_ckv, cache_kpe, cache_ks, *, G):
    db, hm, lora = qa.shape
    rope = qr.shape[-1]
    ps = cache_ckv.shape[2]
    npg = page_table.shape[1]
    G = _pick(npg, G, 1)
    body = functools.partial(_decode_body, G=G, PS=ps, hm=hm, lora=lora)

    def page(g):
        return lambda b, j, pt: (0, pt[b, j * G + g], 0, 0)

    per_b = lambda b, j, pt: (b, 0, 0)
    in_specs = [pl.BlockSpec((1, hm, lora), per_b), pl.BlockSpec((1, hm, rope), per_b),
                pl.BlockSpec((1, 1, lora), per_b), pl.BlockSpec((1, 1, rope), per_b), pl.BlockSpec((1, 1, hm), per_b)]
    in_specs += [pl.BlockSpec((1, 1, ps, lora), page(g)) for g in range(G)]
    in_specs += [pl.BlockSpec((1, 1, ps, rope), page(g)) for g in range(G)]
    in_specs += [pl.BlockSpec((1, 1, ps, hm), page(g)) for g in range(G)]
    grid_spec = pltpu.PrefetchScalarGridSpec(
        num_scalar_prefetch=1, grid=(db, npg // G), in_specs=in_specs,
        out_specs=pl.BlockSpec((1, hm, lora), per_b),
        scratch_shapes=[pltpu.VMEM((1, hm), F32), pltpu.VMEM((1, hm), F32), pltpu.VMEM((hm, lora), F32),
                        pltpu.VMEM((G * ps, lora), BF16), pltpu.VMEM((G * ps, rope), BF16)])
    return pl.pallas_call(
        body, grid_spec=grid_spec, out_shape=jax.ShapeDtypeStruct((db, hm, lora), F32),
        compiler_params=_cp("arbitrary", "arbitrary"), name="mla_decode")(
            page_table, qa, qr, ckv_n, kpe_n, ks_n, *([cache_ckv] * G), *([cache_kpe] * G), *([cache_ks] * G))


def _satt_body(o_ref, wv_ref, a_ref, *, hm, vd):
    for h in range(hm):
        a_ref[:, h * vd:(h + 1) * vd] = _dot(o_ref[h].astype(BF16), wv_ref[h]).astype(a_ref.dtype)


def _satt(o_t, wv_h):
    hm, db, lora = o_t.shape
    vd = wv_h.shape[-1]
    body = functools.partial(_satt_body, hm=hm, vd=vd)
    return pl.pallas_call(
        body, grid=(1,),
        in_specs=[pl.BlockSpec((hm, db, lora), lambda i: (0, 0, 0)), pl.BlockSpec((hm, lora, vd), lambda i: (0, 0, 0))],
        out_specs=pl.BlockSpec((db, hm * vd), lambda i: (0, 0)),
        out_shape=jax.ShapeDtypeStruct((db, hm * vd), BF16),
        compiler_params=_cp("arbitrary"), name="sample_v_up")(o_t, wv_h)


def _rope_tables(pos, rope):
    half = rope // 2
    inv = ROPE_THETA ** (-jnp.arange(half, dtype=jnp.float32) / half)
    ang = pos.astype(jnp.float32)[:, None] * inv[None, :]
    cos, sin = jnp.cos(ang), jnp.sin(ang)
    return jnp.concatenate([cos, cos], axis=-1), jnp.concatenate([-sin, sin], axis=-1)


def _pad_lanes(v):
    return jnp.pad(v.astype(F32), (0, LANES - v.shape[0])).reshape(1, LANES)


def kernel(x_prompt, x_sample, cache_ckv, cache_kpe, cache_kscale, page_table, state_ssm, state_conv, meta_tokens,
           w_norm_mix, w_in, conv_w, conv_b, dt_bias, a_log, d_skip, w_ssm_norm, w_ssm_out, w_kv_norm, q_norm, k_norm,
           w_kv_up, w_attn_out, w_out, w_norm_ffn, w_ffn_in, w_ffn_out):
    batch, seq, d = x_prompt.shape
    db, ts, _ = x_sample.shape
    depth = w_in.shape[0]
    assert depth == 1 and ts == 1, "single layer, one decode token per sequence"
    nm = meta_tokens.shape[0]
    H, P, N = state_ssm.shape[2:]
    di = H * P
    cc = conv_w.shape[2]
    G = (cc - di) // (2 * N)
    gn = G * N
    hm = w_kv_up.shape[2]
    qkh = q_norm.shape[1]
    rope = cache_kpe.shape[3]
    nope = qkh - rope
    vd = w_kv_up.shape[3] - nope
    lora = w_kv_up.shape[1]
    ps = cache_ckv.shape[2]
    past = page_table.shape[1] * ps
    f = w_ffn_out.shape[1]
    scale = qkh ** -0.5
    assert H + rope == LANES and seq % CHUNK == 0 and nm <= CHUNK and P * 2 == LANES

    w0 = w_in[0]
    cuts, acc = [], 0
    for c in (di, cc, H, hm * qkh, lora, rope, d, d):
        cuts.append((acc, acc + c))
        acc += c
    w_z, w_xbc, w_dt, w_q, w_ckv, w_kr, w_ga, w_gs = (w0[:, a:b] for a, b in cuts)
    w_q = w_q.reshape(d, hm, qkh)
    w_qn = w_q[:, :, :nope].reshape(d, hm * nope)
    w_qr = w_q[:, :, nope:].reshape(d, hm * rope)
    w_main = jnp.concatenate([w_z, w_xbc, w_qn, w_qr, w_ga, w_gs], axis=1).astype(BF16)
    w_tail = jnp.concatenate([w_ckv, w_dt, w_kr], axis=1).astype(BF16)
    off_z, off_xs, off_b, off_c = 0, di, 2 * di, 2 * di + gn
    off_qn = di + cc
    off_qr = off_qn + hm * nope
    off_ga = off_qr + hm * rope
    off_gs = off_ga + d
    off_dt = lora

    wkv = w_kv_up[0]
    wk_flat = wkv[:, :, :nope].reshape(lora, hm * nope).astype(BF16)
    wv_flat = wkv[:, :, nope:].reshape(lora, hm * vd).astype(BF16)
    wk_h = jnp.transpose(wkv[:, :, :nope], (1, 0, 2)).astype(BF16)
    wv_h = jnp.transpose(wkv[:, :, nope:], (1, 0, 2)).astype(BF16)
    wa = w_attn_out[0].astype(BF16)
    ws = w_ssm_out[0].astype(BF16)
    wo = w_out[0].astype(BF16)
    wf_in = w_ffn_in[0].astype(BF16)
    wf_out = w_ffn_out[0].astype(BF16)
    qn_w = q_norm[0, :nope].reshape(1, nope)
    qr_w = q_norm[0, nope:].reshape(1, rope)
    kn_w = k_norm[0, :nope].reshape(1, nope)
    kr_w = k_norm[0, nope:].reshape(1, rope)
    wc = w_kv_norm[0].reshape(1, lora)
    cw, cbias = conv_w[0], conv_b[0].reshape(1, cc)
    cwx, cwb, cwc = cw[:, :di], cw[:, di:di + gn], cw[:, di + gn:]
    cbx, cbb, cbc = cbias[:, :di], cbias[:, di:di + gn], cbias[:, di + gn:]
    dtb = _pad_lanes(dt_bias[0])
    alog = _pad_lanes(a_log[0])
    dsk = jnp.repeat(d_skip[0].astype(F32), P).reshape(1, di)
    wsn = w_ssm_norm[0].reshape(1, di)

    cos_p, sin_p = _rope_tables(jnp.arange(nm + seq), rope)
    cos_s, sin_s = _rope_tables(past + jnp.arange(ts), rope)
    small = db + nm
    cos_small = jnp.concatenate([jnp.broadcast_to(cos_s, (db, rope)), cos_p[:nm]], axis=0)
    sin_small = jnp.concatenate([jnp.broadcast_to(sin_s, (db, rope)), sin_p[:nm]], axis=0)

    xp = x_prompt.reshape(batch * seq, d)
    hn_p = _rmsnorm(xp, w_norm_mix[0])
    proj_p = _matmul(hn_p, w_main, BF16, name="in_proj")
    tail_p = _matmul(hn_p, w_tail, F32, name="in_proj_tail")
    xsm = jnp.concatenate([x_sample.reshape(db, d), meta_tokens.astype(x_prompt.dtype)], axis=0)
    hn_s = _rmsnorm(xsm, w_norm_mix[0])
    proj_s = _matmul(hn_s, w_main, F32, name="in_proj_small")
    tail_s = _matmul(hn_s, w_tail, F32, name="in_proj_tail_small")

    lead = CHUNK - nm
    pad = lambda a: jnp.pad(a, ((lead, 0), (0, 0)))
    proj_m, tail_m = pad(proj_s[db:]), pad(tail_s[db:])
    zeros_h = jnp.zeros((N, di), F32)
    zx, zb = jnp.zeros((SUBLANES, di), F32), jnp.zeros((SUBLANES, gn), F32)
    ssd_w = (cwx, cbx, cwb, cbb, cwc, cbc, dtb, alog, dsk, wsn)
    _, _, ht_meta = _ssd(proj_m, tail_m, off_z, off_xs, off_b, off_c, off_dt, *ssd_w, zeros_h, zx, zb, zb,
                         batch=1, nch=1, P=P, N=N, G=G, H=H, lead=lead)
    halo = proj_m[CHUNK - SUBLANES:]
    yn_p, st_p, _ = _ssd(proj_p, tail_p, off_z, off_xs, off_b, off_c, off_dt, *ssd_w, ht_meta[0],
                         halo[:, off_xs:off_xs + di], halo[:, off_b:off_b + gn], halo[:, off_c:off_c + gn],
                         batch=batch, nch=seq // CHUNK, P=P, N=N, G=G, H=H)

    q_p = _prep_q(proj_p, off_qn, off_qr, cos_p[nm:], sin_p[nm:], qn_w, qr_w,
                  hm=hm, nope=nope, rope=rope, scale=scale, seq=seq)
    mla = dict(hm=hm, nope=nope, rope=rope, vd=vd, lora=lora, hs=H)
    ckv_x, kpe_x, ks_x, k_x, v_x = _prep_kv(tail_p, cos_p[nm:], sin_p[nm:], wc, wk_flat, wv_flat, kn_w, kr_w,
                                            seq=seq, **mla)
    ckv_sm, kpe_sm, ks_sm, k_sm, v_sm = _prep_kv(tail_s, cos_small, sin_small, wc, wk_flat, wv_flat, kn_w, kr_w,
                                                 seq=small, **mla)
    att_p = _flash(q_p, k_x.reshape(hm, batch, seq, qkh), v_x.reshape(hm, batch, seq, vd),
                   k_sm[:, db:], v_sm[:, db:], batch=batch, seq=seq)

    u_p = _merge(att_p, yn_p, wa, ws, proj_p, off_ga, off_gs)
    h2_p, hn2_p = _outproj(u_p, wo, xp, w_norm_ffn[0])
    y_p = _ffn_out(_ffn_in(hn2_p, wf_in, f), wf_out, h2_p)

    sconv = state_conv[0].reshape(db, 3 * cc)
    xc_s, xt_s = _spre(proj_s, off_xs, sconv, cw, cbias, db=db, cc=cc)
    dt_s, dec_s = _sdt(tail_s, off_dt, dtb, alog, db=db)
    bm_s = xc_s[:, di:di + gn].reshape(db, G, N)
    cm_s = xc_s[:, di + gn:].reshape(db, G, N)
    st_s, yoff_s = _sstate(dec_s[:, :H], dt_s[:, :H], state_ssm[0], xt_s[:di], bm_s, cm_s, H=H, P=P, N=N, G=G)
    yn_s = _spost(xc_s, yoff_s.reshape(db, di), dt_s, dec_s, proj_s, off_z, dsk, wsn, H=H, P=P, N=N, G=G)

    qa_h, qr_h = _sq(proj_s, off_qn, off_qr, cos_small, sin_small, qn_w, qr_w, kn_w, wk_h,
                     hm=hm, nope=nope, rope=rope, lora=lora, scale=scale, db=db)
    qa_b = jnp.transpose(qa_h, (1, 0, 2))
    qr_b = jnp.transpose(qr_h, (1, 0, 2))
    o_s = _decode(page_table, qa_b, qr_b, ckv_sm[:db].reshape(db, 1, lora), kpe_sm[:db].reshape(db, 1, rope),
                  ks_sm[:db].reshape(db, 1, hm), cache_ckv, cache_kpe, cache_kscale, G=16)
    att_s = _satt(jnp.transpose(o_s, (1, 0, 2)), wv_h)

    u_s = _merge(att_s, yn_s, wa, ws, proj_s, off_ga, off_gs)
    h2_s, hn2_s = _outproj(u_s, wo, x_sample.reshape(db, d), w_norm_ffn[0])
    y_s = _ffn_out(_ffn_in(hn2_s, wf_in, f), wf_out, h2_s)

    def with_meta(xpart, mpart):
        c = xpart.shape[-1]
        m = jnp.broadcast_to(mpart[None], (batch, nm, c))
        return jnp.concatenate([m, xpart.reshape(batch, seq, c)], axis=1)[None]

    conv_p = proj_p.reshape(batch, seq, -1)[:, seq - 3:, off_xs:off_xs + cc].astype(F32)[None]
    conv_s = jnp.concatenate([state_conv[0][:, 1:], proj_s[:db, None, off_xs:off_xs + cc]], axis=1)[None]
    return (y_p.reshape(batch, seq, d), y_s.reshape(db, ts, d),
            with_meta(ckv_x, ckv_sm[db:]), with_meta(kpe_x, kpe_sm[db:]), with_meta(ks_x, ks_sm[db:]),
            st_p.reshape(batch, H, P, N)[None], conv_p,
            ckv_sm[:db].reshape(db, ts, lora)[None], kpe_sm[:db].reshape(db, ts, rope)[None],
            ks_sm[:db].reshape(db, ts, hm)[None], st_s[None], conv_s)
```

```python
import functools

import jax
import jax.numpy as jnp
from jax import lax
from jax.experimental import pallas as pl
from jax.experimental.pallas import tpu as pltpu

EPS = 1e-6
ROPE_THETA = 10000.0
CHUNK = 128
VMEM_LIMIT_BYTES = 56 * 1024 * 1024
LANES = 128
SUBLANES = 8
HALO = 16
LOG2E = 1.4426950408889634
EXP2_CLAMP = 80.0

F32 = jnp.float32
BF16 = jnp.bfloat16


def _cp(*sem):
    return pltpu.CompilerParams(dimension_semantics=sem, vmem_limit_bytes=VMEM_LIMIT_BYTES)


def _pick(n, pref, align=SUBLANES):
    if n <= pref:
        return n
    for t in range(pref - pref % align, 0, -align):
        if n % t == 0:
            return t
    return n


def _sigmoid(x):
    return 1.0 / (1.0 + jnp.exp(-x))


def _silu(x):
    return x * _sigmoid(x)


def _dot(a, b):
    return jnp.dot(a, b, preferred_element_type=F32)


def _dot_nt(a, b):
    return lax.dot_general(a, b, (((1,), (1,)), ((), ())), preferred_element_type=F32)


def _dot_tn(a, b):
    return lax.dot_general(a, b, (((0,), (0,)), ((), ())), preferred_element_type=F32)


def _split3(x):
    hi = x.astype(BF16)
    r = x - hi.astype(F32)
    mid = r.astype(BF16)
    lo = (r - mid.astype(F32)).astype(BF16)
    return hi, mid, lo


def _rope(x, cos2, sin2):
    half = x.shape[-1] // 2
    sw = jnp.concatenate([x[:, half:], x[:, :half]], axis=-1)
    return x * cos2 + sw * sin2


def _rmsnorm_body(x_ref, w_ref, o_ref):
    x = x_ref[...].astype(F32)
    ms = jnp.mean(x * x, axis=-1, keepdims=True)
    o_ref[...] = (x * lax.rsqrt(ms + EPS) * w_ref[...]).astype(o_ref.dtype)


def _rmsnorm(x, w):
    m, d = x.shape
    tm = _pick(m, 512)
    return pl.pallas_call(
        _rmsnorm_body, grid=(m // tm,),
        in_specs=[pl.BlockSpec((tm, d), lambda i: (i, 0)), pl.BlockSpec((1, d), lambda i: (0, 0))],
        out_specs=pl.BlockSpec((tm, d), lambda i: (i, 0)),
        out_shape=jax.ShapeDtypeStruct((m, d), BF16),
        compiler_params=_cp("parallel"), name="rmsnorm")(x, w.reshape(1, d))


def _mm_body(x_ref, w_ref, o_ref):
    o_ref[...] = _dot(x_ref[...], w_ref[...]).astype(o_ref.dtype)


def _matmul(x, w, out_dtype, tm_pref=1024, tn_pref=1024, name="matmul"):
    m, k = x.shape
    n = w.shape[1]
    tm = _pick(m, tm_pref)
    tn = _pick(n, tn_pref, LANES)
    return pl.pallas_call(
        _mm_body, grid=(m // tm, n // tn),
        in_specs=[pl.BlockSpec((tm, k), lambda i, j: (i, 0)), pl.BlockSpec((k, tn), lambda i, j: (0, j))],
        out_specs=pl.BlockSpec((tm, tn), lambda i, j: (i, j)),
        out_shape=jax.ShapeDtypeStruct((m, n), out_dtype),
        compiler_params=_cp("parallel", "arbitrary"), name=name)(x, w)


def _prep_q_body(qn_ref, qr_ref, cos_ref, sin_ref, wn_ref, wr_ref, o_ref, *, hm, nope, rope, scale):
    cos2 = cos_ref[...]
    sin2 = sin_ref[...]
    wn = wn_ref[...]
    wr = wr_ref[...]
    for h in range(hm):
        n = qn_ref[:, h * nope:(h + 1) * nope].astype(F32)
        r = qr_ref[:, h * rope:(h + 1) * rope].astype(F32)
        ms = (jnp.sum(n * n, axis=-1, keepdims=True) + jnp.sum(r * r, axis=-1, keepdims=True)) / (nope + rope)
        inv = lax.rsqrt(ms + EPS)
        rr = _rope(r * inv * wr, cos2, sin2)
        o_ref[h] = (jnp.concatenate([n * inv * wn, rr], axis=-1) * scale).astype(o_ref.dtype)


def _prep_q(proj, off_qn, off_qr, cos2, sin2, wn, wr, *, hm, nope, rope, scale, seq):
    rows = proj.shape[0]
    tm = _pick(seq, 512)
    nseq = seq // tm
    body = functools.partial(_prep_q_body, hm=hm, nope=nope, rope=rope, scale=scale)
    return pl.pallas_call(
        body, grid=(rows // tm,),
        in_specs=[pl.BlockSpec((tm, hm * nope), lambda i: (i, off_qn // (hm * nope))),
                  pl.BlockSpec((tm, hm * rope), lambda i: (i, off_qr // (hm * rope))),
                  pl.BlockSpec((tm, rope), lambda i: (i % nseq, 0)),
                  pl.BlockSpec((tm, rope), lambda i: (i % nseq, 0)),
                  pl.BlockSpec((1, nope), lambda i: (0, 0)),
                  pl.BlockSpec((1, rope), lambda i: (0, 0))],
        out_specs=pl.BlockSpec((hm, tm, nope + rope), lambda i: (0, i, 0)),
        out_shape=jax.ShapeDtypeStruct((hm, rows, nope + rope), BF16),
        compiler_params=_cp("parallel"), name="mla_prep_q")(proj, proj, cos2, sin2, wn, wr)


def _prep_kv_body(ckv_ref, dk_ref, cos_ref, sin_ref, wc_ref, wk_ref, wv_ref, kn_ref, kr_ref,
                  ckv_o, kpe_o, ks_o, k_o, v_o, *, hm, nope, rope, vd, hs):
    x = ckv_ref[...].astype(F32)
    ms = jnp.mean(x * x, axis=-1, keepdims=True)
    ckv = x * lax.rsqrt(ms + EPS) * wc_ref[...]
    ckv_o[...] = ckv
    cb = ckv.astype(BF16)
    kn = _dot(cb, wk_ref[...])
    v = _dot(cb, wv_ref[...])
    kr = dk_ref[:, hs:hs + rope].astype(F32)
    krss = jnp.sum(kr * kr, axis=-1, keepdims=True)
    kpe = _rope(kr * kr_ref[...], cos_ref[...], sin_ref[...])
    kpe_o[...] = kpe
    tm = x.shape[0]
    lane = lax.broadcasted_iota(jnp.int32, (tm, hm), 1)
    ks_all = jnp.zeros((tm, hm), F32)
    for h in range(hm):
        knh = kn[:, h * nope:(h + 1) * nope]
        ss = jnp.sum(knh * knh, axis=-1, keepdims=True) + krss
        ksh = lax.rsqrt(ss / (nope + rope) + EPS)
        ks_all = jnp.where(lane == h, ksh, ks_all)
        k_o[h] = (jnp.concatenate([knh * kn_ref[...], kpe], axis=-1) * ksh).astype(k_o.dtype)
        v_o[h] = v[:, h * vd:(h + 1) * vd].astype(v_o.dtype)
    ks_o[...] = ks_all


def _prep_kv(tail, cos2, sin2, wc, wk, wv, kn_w, kr_w, *, hm, nope, rope, vd, lora, hs, seq):
    rows = tail.shape[0]
    tm = _pick(seq, 512)
    nseq = seq // tm
    body = functools.partial(_prep_kv_body, hm=hm, nope=nope, rope=rope, vd=vd, hs=hs)
    full = lambda i: (0, 0)
    return pl.pallas_call(
        body, grid=(rows // tm,),
        in_specs=[pl.BlockSpec((tm, lora), lambda i: (i, 0)),
                  pl.BlockSpec((tm, hs + rope), lambda i: (i, lora // (hs + rope))),
                  pl.BlockSpec((tm, rope), lambda i: (i % nseq, 0)),
                  pl.BlockSpec((tm, rope), lambda i: (i % nseq, 0)),
                  pl.BlockSpec((1, lora), full),
                  pl.BlockSpec((lora, hm * nope), full),
                  pl.BlockSpec((lora, hm * vd), full),
                  pl.BlockSpec((1, nope), full),
                  pl.BlockSpec((1, rope), full)],
        out_specs=[pl.BlockSpec((tm, lora), lambda i: (i, 0)),
                   pl.BlockSpec((tm, rope), lambda i: (i, 0)),
                   pl.BlockSpec((tm, hm), lambda i: (i, 0)),
                   pl.BlockSpec((hm, tm, nope + rope), lambda i: (0, i, 0)),
                   pl.BlockSpec((hm, tm, vd), lambda i: (0, i, 0))],
        out_shape=[jax.ShapeDtypeStruct((rows, lora), F32),
                   jax.ShapeDtypeStruct((rows, rope), F32),
                   jax.ShapeDtypeStruct((rows, hm), F32),
                   jax.ShapeDtypeStruct((hm, rows, nope + rope), BF16),
                   jax.ShapeDtypeStruct((hm, rows, vd), BF16)],
        compiler_params=_cp("parallel"), name="mla_prep_kv")(tail, tail, cos2, sin2, wc, wk, wv, kn_w, kr_w)


def _flash_body(q_ref, kx_ref, vx_ref, km_ref, vm_ref, o_ref, *, tq, nq):
    km = km_ref[0]
    vm = vm_ref[0]
    row = lax.broadcasted_iota(jnp.int32, (tq, tq), 0)
    col = lax.broadcasted_iota(jnp.int32, (tq, tq), 1)
    for i in range(nq):
        q = q_ref[0, 0, i * tq:(i + 1) * tq, :]
        s = _dot_nt(q, km)
        m = jnp.max(s, axis=-1, keepdims=True)
        p = jnp.exp(s - m)
        l = jnp.sum(p, axis=-1, keepdims=True)
        acc = _dot(p.astype(BF16), vm)
        for j in range(i + 1):
            s = _dot_nt(q, kx_ref[0, 0, j * tq:(j + 1) * tq, :])
            if j == i:
                s = jnp.where(col <= row, s, -jnp.inf)
            m_new = jnp.maximum(m, jnp.max(s, axis=-1, keepdims=True))
            corr = jnp.exp(m - m_new)
            p = jnp.exp(s - m_new)
            l = l * corr + jnp.sum(p, axis=-1, keepdims=True)
            acc = acc * corr + _dot(p.astype(BF16), vx_ref[0, 0, j * tq:(j + 1) * tq, :])
            m = m_new
        o_ref[i * tq:(i + 1) * tq, :] = (acc / l).astype(o_ref.dtype)


def _flash(q, kx, vx, km, vm):
    hm, batch, seq, dk = q.shape
    vd = vx.shape[-1]
    nm = km.shape[1]
    tq = _pick(seq, 512)
    body = functools.partial(_flash_body, tq=tq, nq=seq // tq)
    per_bh = lambda b, h: (h, b, 0, 0)
    return pl.pallas_call(
        body, grid=(batch, hm),
        in_specs=[pl.BlockSpec((1, 1, seq, dk), per_bh),
                  pl.BlockSpec((1, 1, seq, dk), per_bh),
                  pl.BlockSpec((1, 1, seq, vd), per_bh),
                  pl.BlockSpec((1, nm, dk), lambda b, h: (h, 0, 0)),
                  pl.BlockSpec((1, nm, vd), lambda b, h: (h, 0, 0))],
        out_specs=pl.BlockSpec((seq, vd), lambda b, h: (b, h)),
        out_shape=jax.ShapeDtypeStruct((batch * seq, hm * vd), BF16),
        compiler_params=_cp("parallel", "arbitrary"), name="mla_flash")(q, kx, vx, km, vm)


def _ssd_body(z_ref, xs_ref, b_ref, c_ref, dk_ref, cwx_ref, cbx_ref, cwb_ref, cbb_ref, cwc_ref, cbc_ref,
              dtb_ref, alog_ref, dsk_ref, wn_ref, h0_ref, hx_ref, hb_ref, hc_ref,
              yn_ref, st_ref, ht_ref,
              hs, bufx, bufb, bufc, cs_s, rt_s, *, L, P, N, gs, e, lead, nch, ng):
    c = pl.program_id(1)
    gi = pl.program_id(2)
    nh = gs * e

    @pl.when(c == 0)
    def _():
        hs[gi] = h0_ref[...]
        bufx[gi] = hx_ref[...].astype(BF16)
        bufb[gi] = hb_ref[...].astype(BF16)
        bufc[gi] = hc_ref[...].astype(BF16)

    live = lax.broadcasted_iota(jnp.int32, (L, 1), 0) >= lead

    @pl.when(gi == 0)
    def _():
        x = dk_ref[...].astype(F32) + dtb_ref[...]
        dt = jnp.maximum(x, 0.0) + jnp.log(1.0 + jnp.exp(-jnp.abs(x)))
        if lead:
            dt = jnp.where(live, dt, 0.0)
        da = dt * (-jnp.exp(alog_ref[...]))
        r = lax.broadcasted_iota(jnp.int32, (L, L), 0)
        s = lax.broadcasted_iota(jnp.int32, (L, L), 1)
        tri = (r >= s).astype(BF16)
        hi, mid, lo = _split3(da)
        cs2 = (_dot(tri, hi) + _dot(tri, mid) + _dot(tri, lo)) * LOG2E
        rt_s[...] = (cs2 - jnp.log(dt) * LOG2E).T
        for k in range(ng):
            cs_s[k] = pltpu.roll(cs2, (LANES - nh * k) % LANES, axis=1) if k else cs2

    t_i = lax.broadcasted_iota(jnp.int32, (L, L + HALO), 0)
    u_i = lax.broadcasted_iota(jnp.int32, (L, L + HALO), 1)
    shifts = jnp.concatenate([(u_i == t_i + (HALO - j)) for j in (1, 2, 3)], axis=0).astype(BF16)

    par = lax.rem(c, 2)
    rd = par * ng + gi
    wr = (1 - par) * ng + gi

    def conv(buf, raw_ref, cw_ref, cb_ref):
        raw = raw_ref[...].astype(BF16)
        taps = _dot(shifts, jnp.concatenate([buf[rd], raw], axis=0))
        out = cb_ref[...] + cw_ref[3:4, :] * raw.astype(F32) + cw_ref[2:3, :] * taps[0:L] \
            + cw_ref[1:2, :] * taps[L:2 * L] + cw_ref[0:1, :] * taps[2 * L:3 * L]
        buf[wr] = raw[L - HALO:L, :]
        out = _silu(out)
        if lead:
            out = jnp.where(live, out, 0.0)
        return out

    xs = conv(bufx, xs_ref, cwx_ref, cbx_ref)
    bm = conv(bufb, b_ref, cwb_ref, cbb_ref)
    cm = conv(bufc, c_ref, cwc_ref, cbc_ref)

    cs2 = cs_s[gi]
    ecs = jnp.exp2(cs2)
    rt = rt_s[pl.ds(pl.multiple_of(gi * nh, SUBLANES), nh), :]
    r = lax.broadcasted_iota(jnp.int32, (L, L), 0)
    s = lax.broadcasted_iota(jnp.int32, (L, L), 1)
    causal = r >= s
    lo_half = lax.broadcasted_iota(jnp.int32, (1, LANES), 1) < P
    hold_all = hs[gi]

    def head(j, cbm, cg):
        ew = jnp.exp2(jnp.minimum(cs2[:, j:j + 1] - rt[j:j + 1, :], EXP2_CLAMP))
        mh = (cbm * ew).astype(BF16)
        ch = (cg * ecs[:, j:j + 1]).astype(BF16)
        return mh, ch, ew[L - 1:L, :], ecs[L - 1:L, j:j + 1]

    ys, hnew = [], []
    for gl in range(gs):
        bg = bm[:, gl * N:(gl + 1) * N]
        cg = cm[:, gl * N:(gl + 1) * N]
        cbm = jnp.where(causal, _dot_nt(cg.astype(BF16), bg.astype(BF16)), 0.0)
        bt = bg.T
        for pr in range(e // 2):
            j0 = gl * e + 2 * pr
            lane0 = (gl * (e // 2) + pr) * LANES
            m0, c0, w0, d0 = head(j0, cbm, cg)
            m1, c1, w1, d1 = head(j0 + 1, cbm, cg)
            xp = xs[:, lane0:lane0 + LANES]
            hold = hold_all[:, lane0:lane0 + LANES]
            x_lo = jnp.where(lo_half, xp, 0.0).astype(BF16)
            x_hi = jnp.where(lo_half, 0.0, xp).astype(BF16)
            h_lo = jnp.where(lo_half, hold, 0.0).astype(BF16)
            h_hi = jnp.where(lo_half, 0.0, hold).astype(BF16)
            ys.append(_dot(jnp.concatenate([m0, c0, m1, c1], axis=1),
                           jnp.concatenate([x_lo, h_lo, x_hi, h_hi], axis=0)))
            a = jnp.concatenate([(bt * w0).astype(BF16), (bt * w1).astype(BF16)], axis=1)
            upd = _dot(a, jnp.concatenate([x_lo, x_hi], axis=0))
            hnew.append(hold * jnp.where(lo_half, d0, d1) + upd)
    h_all = jnp.concatenate(hnew, axis=1)
    hs[gi] = h_all

    y = jnp.concatenate(ys, axis=1)
    y = (y + dsk_ref[...] * xs) * _silu(z_ref[...].astype(F32))
    gw = e * P
    outs = []
    for gl in range(gs):
        yg = y[:, gl * gw:(gl + 1) * gw]
        ms = jnp.mean(yg * yg, axis=-1, keepdims=True)
        outs.append(yg * lax.rsqrt(ms + EPS))
    yn_ref[...] = (jnp.concatenate(outs, axis=1) * wn_ref[...]).astype(yn_ref.dtype)

    @pl.when(c == nch - 1)
    def _():
        st_ref[0] = h_all.T
        ht_ref[0] = h_all


def _ssd(proj, tail, off_z, off_xs, off_b, off_c, off_dt, conv_wx, conv_bx, conv_wb, conv_bb, conv_wc, conv_bc,
         dtb, alog, dsk, wn, h0t, halo_x, halo_b, halo_c, *, batch, nch, P, N, G, H, lead=0, gs=4):
    L = CHUNK
    e = H // G
    di = H * P
    gw = gs * e * P
    gn = gs * N
    ng = G // gs
    body = functools.partial(_ssd_body, L=L, P=P, N=N, gs=gs, e=e, lead=lead, nch=nch, ng=ng)
    row = lambda b, c, g: b * nch + c
    last = lambda c, g: jnp.where(c == nch - 1, g, 0)
    col0 = lambda b, c, g: (0, g)
    fixed = lambda b, c, g: (0, 0)
    in_specs = [
        pl.BlockSpec((L, gw), lambda b, c, g: (row(b, c, g), off_z // gw + g)),
        pl.BlockSpec((L, gw), lambda b, c, g: (row(b, c, g), off_xs // gw + g)),
        pl.BlockSpec((L, gn), lambda b, c, g: (row(b, c, g), off_b // gn + g)),
        pl.BlockSpec((L, gn), lambda b, c, g: (row(b, c, g), off_c // gn + g)),
        pl.BlockSpec((L, LANES), lambda b, c, g: (row(b, c, g), off_dt // LANES)),
        pl.BlockSpec((4, gw), col0), pl.BlockSpec((1, gw), col0),
        pl.BlockSpec((4, gn), col0), pl.BlockSpec((1, gn), col0),
        pl.BlockSpec((4, gn), col0), pl.BlockSpec((1, gn), col0),
        pl.BlockSpec((1, LANES), fixed), pl.BlockSpec((1, LANES), fixed),
        pl.BlockSpec((1, gw), col0), pl.BlockSpec((1, gw), col0),
        pl.BlockSpec((N, gw), col0),
        pl.BlockSpec((HALO, gw), col0), pl.BlockSpec((HALO, gn), col0), pl.BlockSpec((HALO, gn), col0),
    ]
    out_specs = [
        pl.BlockSpec((L, gw), lambda b, c, g: (row(b, c, g), g)),
        pl.BlockSpec((1, gw, N), lambda b, c, g: (b, last(c, g), 0)),
        pl.BlockSpec((1, N, gw), lambda b, c, g: (b, 0, last(c, g))),
    ]
    out_shape = [jax.ShapeDtypeStruct((batch * nch * L, di), BF16),
                 jax.ShapeDtypeStruct((batch, di, N), F32),
                 jax.ShapeDtypeStruct((batch, N, di), F32)]
    scratch = [pltpu.VMEM((ng, N, gw), F32),
               pltpu.VMEM((2 * ng, HALO, gw), BF16),
               pltpu.VMEM((2 * ng, HALO, gn), BF16),
               pltpu.VMEM((2 * ng, HALO, gn), BF16),
               pltpu.VMEM((ng, L, LANES), F32),
               pltpu.VMEM((LANES, L), F32)]
    return pl.pallas_call(
        body, grid=(batch, nch, ng), in_specs=in_specs, out_specs=out_specs, out_shape=out_shape,
        scratch_shapes=scratch, compiler_params=_cp("arbitrary", "arbitrary", "arbitrary"), name="ssd_scan")(
            proj, proj, proj, proj, tail, conv_wx, conv_bx, conv_wb, conv_bb, conv_wc, conv_bc,
            dtb, alog, dsk, wn, h0t, halo_x, halo_b, halo_c)


def _merge_body(att_ref, yn_ref, wa_ref, ws_ref, ga_ref, gs_ref, o_ref):
    a = _dot(att_ref[...], wa_ref[...])
    s = _dot(yn_ref[...], ws_ref[...])
    ga = _sigmoid(ga_ref[...].astype(F32))
    gs = _sigmoid(gs_ref[...].astype(F32))
    o_ref[...] = (ga * a + gs * s).astype(o_ref.dtype)


def _merge(att, yn, wa, ws, proj, off_ga, off_gs):
    m, ka = att.shape
    ks = yn.shape[1]
    n = wa.shape[1]
    tm = _pick(m, 1024)
    tn = _pick(n, 512, LANES)
    return pl.pallas_call(
        _merge_body, grid=(m // tm, n // tn),
        in_specs=[pl.BlockSpec((tm, ka), lambda i, j: (i, 0)),
                  pl.BlockSpec((tm, ks), lambda i, j: (i, 0)),
                  pl.BlockSpec((ka, tn), lambda i, j: (0, j)),
                  pl.BlockSpec((ks, tn), lambda i, j: (0, j)),
                  pl.BlockSpec((tm, tn), lambda i, j: (i, off_ga // tn + j)),
                  pl.BlockSpec((tm, tn), lambda i, j: (i, off_gs // tn + j))],
        out_specs=pl.BlockSpec((tm, tn), lambda i, j: (i, j)),
        out_shape=jax.ShapeDtypeStruct((m, n), BF16),
        compiler_params=_cp("parallel", "arbitrary"), name="gated_merge")(att, yn, wa, ws, proj, proj)


def _outproj_body(u_ref, wo_ref, h_ref, wn_ref, h2_ref, hn_ref):
    h2 = h_ref[...].astype(F32) + _dot(u_ref[...], wo_ref[...])
    h2_ref[...] = h2
    ms = jnp.mean(h2 * h2, axis=-1, keepdims=True)
    hn_ref[...] = (h2 * lax.rsqrt(ms + EPS) * wn_ref[...]).astype(hn_ref.dtype)


def _outproj(u, wo, h, wn):
    m, d = h.shape
    tm = _pick(m, 512)
    return pl.pallas_call(
        _outproj_body, grid=(m // tm,),
        in_specs=[pl.BlockSpec((tm, d), lambda i: (i, 0)),
                  pl.BlockSpec((d, d), lambda i: (0, 0)),
                  pl.BlockSpec((tm, d), lambda i: (i, 0)),
                  pl.BlockSpec((1, d), lambda i: (0, 0))],
        out_specs=[pl.BlockSpec((tm, d), lambda i: (i, 0)), pl.BlockSpec((tm, d), lambda i: (i, 0))],
        out_shape=[jax.ShapeDtypeStruct((m, d), F32), jax.ShapeDtypeStruct((m, d), BF16)],
        compiler_params=_cp("parallel"), name="out_proj")(u, wo, h, wn.reshape(1, d))


def _ffn_in_body(x_ref, wg_ref, wu_ref, o_ref):
    x = x_ref[...]
    o_ref[...] = (_silu(_dot(x, wg_ref[...])) * _dot(x, wu_ref[...])).astype(o_ref.dtype)


def _ffn_in(x, w, f):
    m, d = x.shape
    tm = _pick(m, 2048)
    tn = _pick(f, 512, LANES)
    nf = f // tn
    return pl.pallas_call(
        _ffn_in_body, grid=(m // tm, nf),
        in_specs=[pl.BlockSpec((tm, d), lambda i, j: (i, 0)),
                  pl.BlockSpec((d, tn), lambda i, j: (0, j)),
                  pl.BlockSpec((d, tn), lambda i, j: (0, nf + j))],
        out_specs=pl.BlockSpec((tm, tn), lambda i, j: (i, j)),
        out_shape=jax.ShapeDtypeStruct((m, f), BF16),
        compiler_params=_cp("parallel", "arbitrary"), name="ffn_in")(x, w, w)


def _ffn_out_body(a_ref, wd_ref, h_ref, o_ref):
    o_ref[...] = h_ref[...] + _dot(a_ref[...], wd_ref[...])


def _ffn_out(a, wd, h):
    m, f = a.shape
    d = wd.shape[1]
    tm = _pick(m, 1024)
    tn = _pick(d, 512, LANES)
    return pl.pallas_call(
        _ffn_out_body, grid=(m // tm, d // tn),
        in_specs=[pl.BlockSpec((tm, f), lambda i, j: (i, 0)),
                  pl.BlockSpec((f, tn), lambda i, j: (0, j)),
                  pl.BlockSpec((tm, tn), lambda i, j: (i, j))],
        out_specs=pl.BlockSpec((tm, tn), lambda i, j: (i, j)),
        out_shape=jax.ShapeDtypeStruct((m, d), F32),
        compiler_params=_cp("parallel", "arbitrary"), name="ffn_out")(a, wd, h)


def _spre_body(x_ref, s0_ref, s1_ref, s2_ref, cw_ref, cb_ref, xc_ref, xt_ref):
    x = x_ref[...].astype(F32)
    out = cb_ref[...] + cw_ref[0:1, :] * s0_ref[...] + cw_ref[1:2, :] * s1_ref[...] \
        + cw_ref[2:3, :] * s2_ref[...] + cw_ref[3:4, :] * x
    xc = _silu(out)
    xc_ref[...] = xc
    xt_ref[...] = xc.T.astype(xt_ref.dtype)


def _spre(proj, off_xbc, sconv, conv_w, conv_b, *, db, cc):
    tn = _pick(cc, 2048, LANES)
    nb = cc // tn
    return pl.pallas_call(
        _spre_body, grid=(nb,),
        in_specs=[pl.BlockSpec((db, tn), lambda j: (0, off_xbc // tn + j)),
                  pl.BlockSpec((db, tn), lambda j: (0, j)),
                  pl.BlockSpec((db, tn), lambda j: (0, nb + j)),
                  pl.BlockSpec((db, tn), lambda j: (0, 2 * nb + j)),
                  pl.BlockSpec((4, tn), lambda j: (0, j)),
                  pl.BlockSpec((1, tn), lambda j: (0, j))],
        out_specs=[pl.BlockSpec((db, tn), lambda j: (0, j)), pl.BlockSpec((tn, db), lambda j: (j, 0))],
        out_shape=[jax.ShapeDtypeStruct((db, cc), F32), jax.ShapeDtypeStruct((cc, db), BF16)],
        compiler_params=_cp("parallel"), name="sample_conv")(proj, sconv, sconv, sconv, conv_w, conv_b)


def _sdt_body(dk_ref, dtb_ref, alog_ref, dt_ref, dec_ref):
    x = dk_ref[...].astype(F32) + dtb_ref[...]
    dt = jnp.maximum(x, 0.0) + jnp.log(1.0 + jnp.exp(-jnp.abs(x)))
    dt_ref[...] = dt
    dec_ref[...] = jnp.exp(dt * (-jnp.exp(alog_ref[...])))


def _sdt(tail, off_dt, dtb, alog, *, db):
    blk = pl.BlockSpec((db, LANES), lambda i: (0, 0))
    return pl.pallas_call(
        _sdt_body, grid=(1,),
        in_specs=[pl.BlockSpec((db, LANES), lambda i: (0, off_dt // LANES)),
                  pl.BlockSpec((1, LANES), lambda i: (0, 0)), pl.BlockSpec((1, LANES), lambda i: (0, 0))],
        out_specs=[blk, blk],
        out_shape=[jax.ShapeDtypeStruct((db, LANES), F32), jax.ShapeDtypeStruct((db, LANES), F32)],
        compiler_params=_cp("arbitrary"), name="sample_dt")(tail, dtb, alog)


def _sstate_body(dec_ref, dts_ref, st_ref, xt_ref, b_ref, c_ref, sto_ref, yo_ref, *, H, P, N, G, db):
    b = pl.program_id(0)
    e = H // G
    di = H * P
    bb = b_ref[0]
    h0 = st_ref[0].reshape(di, N)
    y = _dot_nt(c_ref[0].astype(BF16), h0.astype(BF16))
    grow = lax.broadcasted_iota(jnp.int32, (G, di), 0)
    gcol = lax.broadcasted_iota(jnp.int32, (G, di), 1) // (e * P)
    yo_ref[0] = jnp.sum(jnp.where(grow == gcol, y, 0.0), axis=0, keepdims=True)
    onehot = lax.broadcasted_iota(jnp.int32, (db, N), 0) == b
    for g in range(G):
        mg = jnp.where(onehot, bb[g:g + 1, :], 0.0).astype(BF16)
        u = _dot(xt_ref[g * e * P:(g + 1) * e * P, :], mg)
        for hh in range(e):
            h = g * e + hh
            sto_ref[0, h] = st_ref[0, h] * dec_ref[b, h] + dts_ref[b, h] * u[hh * P:(hh + 1) * P, :]


def _sstate(dec, dts, state, xt, bm, cm, *, H, P, N, G):
    db = state.shape[0]
    di = H * P
    body = functools.partial(_sstate_body, H=H, P=P, N=N, G=G, db=db)
    smem = pl.BlockSpec(memory_space=pltpu.SMEM)
    return pl.pallas_call(
        body, grid=(db,),
        in_specs=[smem, smem,
                  pl.BlockSpec((1, H, P, N), lambda b: (b, 0, 0, 0)),
                  pl.BlockSpec((di, db), lambda b: (0, 0)),
                  pl.BlockSpec((1, G, N), lambda b: (b, 0, 0)),
                  pl.BlockSpec((1, G, N), lambda b: (b, 0, 0))],
        out_specs=[pl.BlockSpec((1, H, P, N), lambda b: (b, 0, 0, 0)),
                   pl.BlockSpec((1, 1, di), lambda b: (b, 0, 0))],
        out_shape=[jax.ShapeDtypeStruct((db, H, P, N), F32), jax.ShapeDtypeStruct((db, 1, di), F32)],
        compiler_params=_cp("parallel"), name="sample_state")(dec, dts, state, xt, bm, cm)


def _spost_body(xs_ref, b_ref, c_ref, yo_ref, dt_ref, dec_ref, z_ref, dsk_ref, wn_ref, o_ref, *, H, P, N, G):
    e = H // G
    di = H * P
    xs = xs_ref[...]
    hrow = lax.broadcasted_iota(jnp.int32, (LANES, di), 0)
    hcol = lax.broadcasted_iota(jnp.int32, (LANES, di), 1) // P
    expand = (hrow == hcol).astype(BF16)

    def rep(v):
        hi, mid, lo = _split3(v)
        return _dot(hi, expand) + _dot(mid, expand) + _dot(lo, expand)

    dt_rep = rep(dt_ref[...])
    dec_rep = rep(dec_ref[...])
    prod = b_ref[...] * c_ref[...]
    gw = e * P
    outs = []
    for g in range(G):
        cbg = jnp.sum(prod[:, g * N:(g + 1) * N], axis=-1, keepdims=True)
        sl = slice(g * gw, (g + 1) * gw)
        y = cbg * dt_rep[:, sl] * xs[:, sl] + dec_rep[:, sl] * yo_ref[:, sl] + dsk_ref[:, sl] * xs[:, sl]
        y = y * _silu(z_ref[:, sl].astype(F32))
        ms = jnp.mean(y * y, axis=-1, keepdims=True)
        outs.append(y * lax.rsqrt(ms + EPS))
    o_ref[...] = (jnp.concatenate(outs, axis=1) * wn_ref[...]).astype(o_ref.dtype)


def _spost(xc, yoff, dt, dec, proj, off_z, dsk, wn, *, H, P, N, G):
    db = xc.shape[0]
    di = H * P
    gn = G * N
    body = functools.partial(_spost_body, H=H, P=P, N=N, G=G)
    one = lambda i: (0, 0)
    return pl.pallas_call(
        body, grid=(1,),
        in_specs=[pl.BlockSpec((db, di), one),
                  pl.BlockSpec((db, gn), lambda i: (0, di // gn)),
                  pl.BlockSpec((db, gn), lambda i: (0, di // gn + 1)),
                  pl.BlockSpec((db, di), one),
                  pl.BlockSpec((db, LANES), one), pl.BlockSpec((db, LANES), one),
                  pl.BlockSpec((db, di), lambda i: (0, off_z // di)),
                  pl.BlockSpec((1, di), one), pl.BlockSpec((1, di), one)],
        out_specs=pl.BlockSpec((db, di), one),
        out_shape=jax.ShapeDtypeStruct((db, di), BF16),
        compiler_params=_cp("arbitrary"), name="sample_ssm_post")(xc, xc, xc, yoff, dt, dec, proj, dsk, wn)


def _sq_body(qn_ref, qr_ref, cos_ref, sin_ref, wn_ref, wr_ref, kn_ref, wk_ref, qa_ref, qo_ref, *, hm, nope, rope, scale):
    cos2 = cos_ref[...]
    sin2 = sin_ref[...]
    for h in range(hm):
        n = qn_ref[:, h * nope:(h + 1) * nope].astype(F32)
        r = qr_ref[:, h * rope:(h + 1) * rope].astype(F32)
        ms = (jnp.sum(n * n, axis=-1, keepdims=True) + jnp.sum(r * r, axis=-1, keepdims=True)) / (nope + rope)
        inv = lax.rsqrt(ms + EPS)
        rr = _rope(r * inv * wr_ref[...], cos2, sin2)
        qn = (n * inv * wn_ref[...] * kn_ref[...]).astype(BF16)
        qa_ref[h] = (_dot_nt(qn, wk_ref[h]) * scale).astype(qa_ref.dtype)
        qo_ref[h] = (rr * scale).astype(qo_ref.dtype)


def _sq(proj, off_qn, off_qr, cos2, sin2, wn, wr, kn_w, wk_h, *, hm, nope, rope, lora, scale, db):
    body = functools.partial(_sq_body, hm=hm, nope=nope, rope=rope, scale=scale)
    one = lambda i: (0, 0)
    return pl.pallas_call(
        body, grid=(1,),
        in_specs=[pl.BlockSpec((db, hm * nope), lambda i: (0, off_qn // (hm * nope))),
                  pl.BlockSpec((db, hm * rope), lambda i: (0, off_qr // (hm * rope))),
                  pl.BlockSpec((db, rope), one), pl.BlockSpec((db, rope), one),
                  pl.BlockSpec((1, nope), one), pl.BlockSpec((1, rope), one), pl.BlockSpec((1, nope), one),
                  pl.BlockSpec((hm, lora, nope), lambda i: (0, 0, 0))],
        out_specs=[pl.BlockSpec((hm, db, lora), lambda i: (0, 0, 0)), pl.BlockSpec((hm, db, rope), lambda i: (0, 0, 0))],
        out_shape=[jax.ShapeDtypeStruct((hm, db, lora), BF16), jax.ShapeDtypeStruct((hm, db, rope), BF16)],
        compiler_params=_cp("arbitrary"), name="sample_q_absorb")(proj, proj, cos2, sin2, wn, wr, kn_w, wk_h)


def _decode_body(pt_ref, qa_ref, qr_ref, cn_ref, kn_ref, sn_ref, ckv_hbm, kpe_hbm, ks_hbm, o_ref,
                 m_s, l_s, acc_s, cst, kst, sst, sems, *, G, PS, U):
    bu = pl.program_id(0)
    j = pl.program_id(1)
    nj = pl.num_programs(1)
    step = bu * nj + j
    slot = lax.rem(step, 2)

    def page_copies(bb, jj, sl):
        copies = []
        for u in range(U):
            for g in range(G):
                pg = pt_ref[bb * U + u, jj * G + g]
                keys = pl.ds(g * PS, PS)
                copies.append(pltpu.make_async_copy(ckv_hbm.at[0, pg], cst.at[sl, u, keys, :], sems.at[sl, 0]))
                copies.append(pltpu.make_async_copy(kpe_hbm.at[0, pg], kst.at[sl, u, :, keys], sems.at[sl, 1]))
                copies.append(pltpu.make_async_copy(ks_hbm.at[0, pg], sst.at[sl, u, :, keys], sems.at[sl, 2]))
        return copies

    @pl.when(step == 0)
    def _():
        for cp in page_copies(bu, j, slot):
            cp.start()

    @pl.when(step + 1 < pl.num_programs(0) * nj)
    def _():
        wrap = j == nj - 1
        for cp in page_copies(jnp.where(wrap, bu + 1, bu), jnp.where(wrap, 0, j + 1), 1 - slot):
            cp.start()

    for cp in page_copies(bu, j, slot):
        cp.wait()

    @pl.when(j == 0)
    def _():
        m_s[...] = jnp.full(m_s.shape, -jnp.inf, F32)
        l_s[...] = jnp.zeros(l_s.shape, F32)
        acc_s[...] = jnp.zeros(acc_s.shape, F32)

    for u in range(U):
        cb = cst[slot, u].astype(BF16)
        qa = qa_ref[u]
        qr = qr_ref[u]
        sc = (_dot_nt(qa, cb) + _dot(qr, kst[slot, u].astype(BF16))) * sst[slot, u]
        m_old = m_s[u, :, 0:1]
        m_new = jnp.maximum(m_old, jnp.max(sc, axis=1, keepdims=True))
        corr = jnp.exp(m_old - m_new)
        p = jnp.exp(sc - m_new)
        l_new = l_s[u, :, 0:1] * corr + jnp.sum(p, axis=1, keepdims=True)
        acc = acc_s[u] * corr + _dot(p.astype(BF16), cb)
        m_s[u] = jnp.broadcast_to(m_new, m_s.shape[1:])
        l_s[u] = jnp.broadcast_to(l_new, l_s.shape[1:])
        acc_s[u] = acc

    @pl.when(j == nj - 1)
    def _():
        for u in range(U):
            cn = cn_ref[u].astype(F32)
            kn = kn_ref[u].astype(F32)
            scn = (jnp.sum(qa_ref[u].astype(F32) * cn, axis=1, keepdims=True)
                   + jnp.sum(qr_ref[u].astype(F32) * kn, axis=1, keepdims=True)) * sn_ref[u]
            m_new = m_s[u, :, 0:1]
            m_f = jnp.maximum(m_new, scn)
            corr2 = jnp.exp(m_new - m_f)
            pn = jnp.exp(scn - m_f)
            l_f = l_s[u, :, 0:1] * corr2 + pn
            o_ref[u] = (acc_s[u] * corr2 + pn * cn) / l_f


def _decode(page_table, qa, qr, ckv_n, kpe_n, ks_n, cache_ckv, cache_kpe_t, cache_ks_t, *, G, U):
    db, hm, lora = qa.shape
    rope = qr.shape[-1]
    ps = cache_ckv.shape[2]
    npg = page_table.shape[1]
    G = _pick(npg, G, 1)
    U = _pick(db, U, 1)
    body = functools.partial(_decode_body, G=G, PS=ps, U=U)
    per_b = lambda b, j, pt: (b, 0, 0)
    hbm = pl.BlockSpec(memory_space=pl.ANY)
    grid_spec = pltpu.PrefetchScalarGridSpec(
        num_scalar_prefetch=1, grid=(db // U, npg // G),
        in_specs=[pl.BlockSpec((U, hm, lora), per_b), pl.BlockSpec((U, hm, rope), per_b),
                  pl.BlockSpec((U, 1, lora), per_b), pl.BlockSpec((U, 1, rope), per_b),
                  pl.BlockSpec((U, hm, 1), per_b), hbm, hbm, hbm],
        out_specs=pl.BlockSpec((U, hm, lora), per_b),
        scratch_shapes=[pltpu.VMEM((U, hm, LANES), F32), pltpu.VMEM((U, hm, LANES), F32),
                        pltpu.VMEM((U, hm, lora), F32),
                        pltpu.VMEM((2, U, G * ps, lora), F32), pltpu.VMEM((2, U, rope, G * ps), F32),
                        pltpu.VMEM((2, U, hm, G * ps), F32), pltpu.SemaphoreType.DMA((2, 3))])
    return pl.pallas_call(
        body, grid_spec=grid_spec, out_shape=jax.ShapeDtypeStruct((db, hm, lora), F32),
        compiler_params=_cp("arbitrary", "arbitrary"), name="mla_decode")(
            page_table, qa, qr, ckv_n, kpe_n, ks_n, cache_ckv, cache_kpe_t, cache_ks_t)


def _satt_body(o_ref, wv_ref, a_ref, *, hm, vd):
    for h in range(hm):
        a_ref[:, h * vd:(h + 1) * vd] = _dot(o_ref[h].astype(BF16), wv_ref[h]).astype(a_ref.dtype)


def _satt(o_t, wv_h):
    hm, db, lora = o_t.shape
    vd = wv_h.shape[-1]
    body = functools.partial(_satt_body, hm=hm, vd=vd)
    return pl.pallas_call(
        body, grid=(1,),
        in_specs=[pl.BlockSpec((hm, db, lora), lambda i: (0, 0, 0)), pl.BlockSpec((hm, lora, vd), lambda i: (0, 0, 0))],
        out_specs=pl.BlockSpec((db, hm * vd), lambda i: (0, 0)),
        out_shape=jax.ShapeDtypeStruct((db, hm * vd), BF16),
        compiler_params=_cp("arbitrary"), name="sample_v_up")(o_t, wv_h)


def _rope_tables(pos, rope):
    half = rope // 2
    inv = ROPE_THETA ** (-jnp.arange(half, dtype=jnp.float32) / half)
    ang = pos.astype(jnp.float32)[:, None] * inv[None, :]
    cos, sin = jnp.cos(ang), jnp.sin(ang)
    return jnp.concatenate([cos, cos], axis=-1), jnp.concatenate([-sin, sin], axis=-1)


def _pad_lanes(v):
    return jnp.pad(v.astype(F32), (0, LANES - v.shape[0])).reshape(1, LANES)


def kernel(x_prompt, x_sample, cache_ckv, cache_kpe, cache_kscale, page_table, state_ssm, state_conv, meta_tokens,
           w_norm_mix, w_in, conv_w, conv_b, dt_bias, a_log, d_skip, w_ssm_norm, w_ssm_out, w_kv_norm, q_norm, k_norm,
           w_kv_up, w_attn_out, w_out, w_norm_ffn, w_ffn_in, w_ffn_out):
    batch, seq, d = x_prompt.shape
    db, ts, _ = x_sample.shape
    depth = w_in.shape[0]
    assert depth == 1 and ts == 1, "single layer, one decode token per sequence"
    nm = meta_tokens.shape[0]
    H, P, N = state_ssm.shape[2:]
    di = H * P
    cc = conv_w.shape[2]
    G = (cc - di) // (2 * N)
    gn = G * N
    hm = w_kv_up.shape[2]
    qkh = q_norm.shape[1]
    rope = cache_kpe.shape[3]
    nope = qkh - rope
    vd = w_kv_up.shape[3] - nope
    lora = w_kv_up.shape[1]
    ps = cache_ckv.shape[2]
    past = page_table.shape[1] * ps
    f = w_ffn_out.shape[1]
    scale = qkh ** -0.5
    assert H + rope == LANES and seq % CHUNK == 0 and nm <= CHUNK and P * 2 == LANES

    w0 = w_in[0]
    cuts, acc = [], 0
    for c in (di, cc, H, hm * qkh, lora, rope, d, d):
        cuts.append((acc, acc + c))
        acc += c
    w_z, w_xbc, w_dt, w_q, w_ckv, w_kr, w_ga, w_gs = (w0[:, a:b] for a, b in cuts)
    w_q = w_q.reshape(d, hm, qkh)
    w_qn = w_q[:, :, :nope].reshape(d, hm * nope)
    w_qr = w_q[:, :, nope:].reshape(d, hm * rope)
    w_main = jnp.concatenate([w_z, w_xbc, w_qn, w_qr, w_ga, w_gs], axis=1).astype(BF16)
    w_tail = jnp.concatenate([w_ckv, w_dt, w_kr], axis=1).astype(BF16)
    off_z, off_xs, off_b, off_c = 0, di, 2 * di, 2 * di + gn
    off_qn = di + cc
    off_qr = off_qn + hm * nope
    off_ga = off_qr + hm * rope
    off_gs = off_ga + d
    off_dt = lora

    wkv = w_kv_up[0]
    wk_flat = wkv[:, :, :nope].reshape(lora, hm * nope).astype(BF16)
    wv_flat = wkv[:, :, nope:].reshape(lora, hm * vd).astype(BF16)
    wk_h = jnp.transpose(wkv[:, :, :nope], (1, 0, 2)).astype(BF16)
    wv_h = jnp.transpose(wkv[:, :, nope:], (1, 0, 2)).astype(BF16)
    wa = w_attn_out[0].astype(BF16)
    ws = w_ssm_out[0].astype(BF16)
    wo = w_out[0].astype(BF16)
    wf_in = w_ffn_in[0].astype(BF16)
    wf_out = w_ffn_out[0].astype(BF16)
    qn_w = q_norm[0, :nope].reshape(1, nope)
    qr_w = q_norm[0, nope:].reshape(1, rope)
    kn_w = k_norm[0, :nope].reshape(1, nope)
    kr_w = k_norm[0, nope:].reshape(1, rope)
    wc = w_kv_norm[0].reshape(1, lora)
    cw, cbias = conv_w[0], conv_b[0].reshape(1, cc)
    cwx, cwb, cwc = cw[:, :di], cw[:, di:di + gn], cw[:, di + gn:]
    cbx, cbb, cbc = cbias[:, :di], cbias[:, di:di + gn], cbias[:, di + gn:]
    dtb = _pad_lanes(dt_bias[0])
    alog = _pad_lanes(a_log[0])
    dsk = jnp.repeat(d_skip[0].astype(F32), P).reshape(1, di)
    wsn = w_ssm_norm[0].reshape(1, di)

    cos_p, sin_p = _rope_tables(jnp.arange(nm + seq), rope)
    cos_s, sin_s = _rope_tables(past + jnp.arange(ts), rope)
    small = db + nm
    cos_small = jnp.concatenate([jnp.broadcast_to(cos_s, (db, rope)), cos_p[:nm]], axis=0)
    sin_small = jnp.concatenate([jnp.broadcast_to(sin_s, (db, rope)), sin_p[:nm]], axis=0)

    xp = x_prompt.reshape(batch * seq, d)
    hn_p = _rmsnorm(xp, w_norm_mix[0])
    proj_p = _matmul(hn_p, w_main, BF16, name="in_proj")
    tail_p = _matmul(hn_p, w_tail, F32, name="in_proj_tail")
    xsm = jnp.concatenate([x_sample.reshape(db, d), meta_tokens.astype(x_prompt.dtype)], axis=0)
    hn_s = _rmsnorm(xsm, w_norm_mix[0])
    proj_s = _matmul(hn_s, w_main, F32, name="in_proj_small")
    tail_s = _matmul(hn_s, w_tail, F32, name="in_proj_tail_small")

    lead = CHUNK - nm
    pad = lambda a: jnp.pad(a, ((lead, 0), (0, 0)))
    proj_m, tail_m = pad(proj_s[db:]), pad(tail_s[db:])
    zeros_h = jnp.zeros((N, di), F32)
    zx, zb = jnp.zeros((HALO, di), F32), jnp.zeros((HALO, gn), F32)
    ssd_w = (cwx, cbx, cwb, cbb, cwc, cbc, dtb, alog, dsk, wsn)
    _, _, ht_meta = _ssd(proj_m, tail_m, off_z, off_xs, off_b, off_c, off_dt, *ssd_w, zeros_h, zx, zb, zb,
                         batch=1, nch=1, P=P, N=N, G=G, H=H, lead=lead)
    halo = proj_m[CHUNK - HALO:]
    yn_p, st_p, _ = _ssd(proj_p, tail_p, off_z, off_xs, off_b, off_c, off_dt, *ssd_w, ht_meta[0],
                         halo[:, off_xs:off_xs + di], halo[:, off_b:off_b + gn], halo[:, off_c:off_c + gn],
                         batch=batch, nch=seq // CHUNK, P=P, N=N, G=G, H=H)

    q_p = _prep_q(proj_p, off_qn, off_qr, cos_p[nm:], sin_p[nm:], qn_w, qr_w,
                  hm=hm, nope=nope, rope=rope, scale=scale, seq=seq)
    mla = dict(hm=hm, nope=nope, rope=rope, vd=vd, lora=lora, hs=H)
    ckv_x, kpe_x, ks_x, k_x, v_x = _prep_kv(tail_p, cos_p[nm:], sin_p[nm:], wc, wk_flat, wv_flat, kn_w, kr_w,
                                            seq=seq, **mla)
    ckv_sm, kpe_sm, ks_sm, k_sm, v_sm = _prep_kv(tail_s, cos_small, sin_small, wc, wk_flat, wv_flat, kn_w, kr_w,
                                                 seq=small, **mla)
    att_p = _flash(q_p.reshape(hm, batch, seq, qkh), k_x.reshape(hm, batch, seq, qkh),
                   v_x.reshape(hm, batch, seq, vd), k_sm[:, db:], v_sm[:, db:])

    u_p = _merge(att_p, yn_p, wa, ws, proj_p, off_ga, off_gs)
    h2_p, hn2_p = _outproj(u_p, wo, xp, w_norm_ffn[0])
    y_p = _ffn_out(_ffn_in(hn2_p, wf_in, f), wf_out, h2_p)

    sconv = state_conv[0].reshape(db, 3 * cc)
    xc_s, xt_s = _spre(proj_s, off_xs, sconv, cw, cbias, db=db, cc=cc)
    dt_s, dec_s = _sdt(tail_s, off_dt, dtb, alog, db=db)
    bm_s = xc_s[:, di:di + gn].reshape(db, G, N)
    cm_s = xc_s[:, di + gn:].reshape(db, G, N)
    st_s, yoff_s = _sstate(dec_s[:, :H], dt_s[:, :H], state_ssm[0], xt_s[:di], bm_s, cm_s, H=H, P=P, N=N, G=G)
    yn_s = _spost(xc_s, yoff_s.reshape(db, di), dt_s, dec_s, proj_s, off_z, dsk, wsn, H=H, P=P, N=N, G=G)

    qa_h, qr_h = _sq(proj_s, off_qn, off_qr, cos_small, sin_small, qn_w, qr_w, kn_w, wk_h,
                     hm=hm, nope=nope, rope=rope, lora=lora, scale=scale, db=db)
    qa_b = jnp.transpose(qa_h, (1, 0, 2))
    qr_b = jnp.transpose(qr_h, (1, 0, 2))
    o_s = _decode(page_table, qa_b, qr_b, ckv_sm[:db].reshape(db, 1, lora), kpe_sm[:db].reshape(db, 1, rope),
                  ks_sm[:db].reshape(db, hm, 1), cache_ckv, jnp.swapaxes(cache_kpe, 2, 3),
                  jnp.swapaxes(cache_kscale, 2, 3), G=16, U=2)
    att_s = _satt(jnp.transpose(o_s, (1, 0, 2)), wv_h)

    u_s = _merge(att_s, yn_s, wa, ws, proj_s, off_ga, off_gs)
    h2_s, hn2_s = _outproj(u_s, wo, x_sample.reshape(db, d), w_norm_ffn[0])
    y_s = _ffn_out(_ffn_in(hn2_s, wf_in, f), wf_out, h2_s)

    def with_meta(xpart, mpart):
        c = xpart.shape[-1]
        m = jnp.broadcast_to(mpart[None], (batch, nm, c))
        return jnp.concatenate([m, xpart.reshape(batch, seq, c)], axis=1)[None]

    conv_p = proj_p.reshape(batch, seq, -1)[:, seq - 3:, off_xs:off_xs + cc].astype(F32)[None]
    conv_s = jnp.concatenate([state_conv[0][:, 1:], proj_s[:db, None, off_xs:off_xs + cc]], axis=1)[None]
    return (y_p.reshape(batch, seq, d), y_s.reshape(db, ts, d),
            with_meta(ckv_x, ckv_sm[db:]), with_meta(kpe_x, kpe_sm[db:]), with_meta(ks_x, ks_sm[db:]),
            st_p.reshape(batch, H, P, N)[None], conv_p,
            ckv_sm[:db].reshape(db, ts, lora)[None], kpe_sm[:db].reshape(db, ts, rope)[None],
            ks_sm[:db].reshape(db, ts, hm)[None], st_s[None], conv_s)
```

```python
import functools

import jax
import jax.numpy as jnp
from jax import lax
from jax.experimental import pallas as pl
from jax.experimental.pallas import tpu as pltpu

EPS = 1e-6
ROPE_THETA = 10000.0
CHUNK = 128
VMEM_LIMIT_BYTES = 56 * 1024 * 1024
LANES = 128
SUBLANES = 8
HALO = 16
LOG2E = 1.4426950408889634
EXP2_CLAMP = 80.0

F32 = jnp.float32
BF16 = jnp.bfloat16


def _cp(*sem):
    return pltpu.CompilerParams(dimension_semantics=sem, vmem_limit_bytes=VMEM_LIMIT_BYTES)


def _pick(n, pref, align=SUBLANES):
    if n <= pref:
        return n
    for t in range(pref - pref % align, 0, -align):
        if n % t == 0:
            return t
    return n


def _sigmoid(x):
    return 1.0 / (1.0 + jnp.exp(-x))


def _silu(x):
    return x * _sigmoid(x)


def _dot(a, b):
    return jnp.dot(a, b, preferred_element_type=F32)


def _dot_nt(a, b):
    return lax.dot_general(a, b, (((1,), (1,)), ((), ())), preferred_element_type=F32)


def _dot_tn(a, b):
    return lax.dot_general(a, b, (((0,), (0,)), ((), ())), preferred_element_type=F32)


def _split3(x):
    hi = x.astype(BF16)
    r = x - hi.astype(F32)
    mid = r.astype(BF16)
    lo = (r - mid.astype(F32)).astype(BF16)
    return hi, mid, lo


def _rope(x, cos2, sin2):
    half = x.shape[-1] // 2
    sw = jnp.concatenate([x[:, half:], x[:, :half]], axis=-1)
    return x * cos2 + sw * sin2


def _rmsnorm_body(x_ref, w_ref, o_ref):
    x = x_ref[...].astype(F32)
    ms = jnp.mean(x * x, axis=-1, keepdims=True)
    o_ref[...] = (x * lax.rsqrt(ms + EPS) * w_ref[...]).astype(o_ref.dtype)


def _rmsnorm(x, w):
    m, d = x.shape
    tm = _pick(m, 512)
    return pl.pallas_call(
        _rmsnorm_body, grid=(m // tm,),
        in_specs=[pl.BlockSpec((tm, d), lambda i: (i, 0)), pl.BlockSpec((1, d), lambda i: (0, 0))],
        out_specs=pl.BlockSpec((tm, d), lambda i: (i, 0)),
        out_shape=jax.ShapeDtypeStruct((m, d), BF16),
        compiler_params=_cp("parallel"), name="rmsnorm")(x, w.reshape(1, d))


def _mm_body(x_ref, w_ref, o_ref):
    o_ref[...] = _dot(x_ref[...], w_ref[...]).astype(o_ref.dtype)


def _matmul(x, w, out_dtype, tm_pref=1024, tn_pref=1024, name="matmul"):
    m, k = x.shape
    n = w.shape[1]
    tm = _pick(m, tm_pref)
    tn = _pick(n, tn_pref, LANES)
    return pl.pallas_call(
        _mm_body, grid=(m // tm, n // tn),
        in_specs=[pl.BlockSpec((tm, k), lambda i, j: (i, 0)), pl.BlockSpec((k, tn), lambda i, j: (0, j))],
        out_specs=pl.BlockSpec((tm, tn), lambda i, j: (i, j)),
        out_shape=jax.ShapeDtypeStruct((m, n), out_dtype),
        compiler_params=_cp("parallel", "arbitrary"), name=name)(x, w)


def _prep_q_body(qn_ref, qr_ref, cos_ref, sin_ref, wn_ref, wr_ref, o_ref, *, hm, nope, rope, scale):
    half = rope // 2
    per_vreg = LANES // rope
    r_all = qr_ref[...].astype(F32)
    wr_all = r_all.shape[1]
    lane_all = lax.broadcasted_iota(jnp.int32, (1, wr_all), 1)
    first_half = lax.rem(lane_all, rope) < half
    sw_all = jnp.where(first_half, pltpu.roll(r_all, wr_all - half, axis=1), pltpu.roll(r_all, half, axis=1))
    cos_t = jnp.concatenate([cos_ref[...]] * per_vreg, axis=1)
    sin_t = jnp.concatenate([sin_ref[...]] * per_vreg, axis=1)
    wr_t = jnp.concatenate([wr_ref[...]] * per_vreg, axis=1)
    wr_sw_t = jnp.concatenate([jnp.concatenate([wr_ref[:, half:], wr_ref[:, :half]], axis=1)] * per_vreg, axis=1)
    wn = wn_ref[...]
    rsq = r_all * r_all
    lane = lax.broadcasted_iota(jnp.int32, (1, LANES), 1)
    for c in range(hm // per_vreg):
        cols = slice(c * LANES, (c + 1) * LANES)
        heads = [c * per_vreg + u for u in range(per_vreg)]
        invs = []
        for u, h in enumerate(heads):
            n = qn_ref[:, h * nope:(h + 1) * nope].astype(F32)
            seg = rsq[:, c * LANES + u * rope:c * LANES + (u + 1) * rope]
            ms = (jnp.sum(n * n, axis=-1, keepdims=True) + jnp.sum(seg, axis=-1, keepdims=True)) / (nope + rope)
            invs.append(lax.rsqrt(ms + EPS))
        inv_t = invs[0]
        for u in range(1, per_vreg):
            inv_t = jnp.where(lane < u * rope, inv_t, invs[u])
        rr = r_all[:, cols] * inv_t * wr_t * cos_t + sw_all[:, cols] * inv_t * wr_sw_t * sin_t
        for u, h in enumerate(heads):
            n = qn_ref[:, h * nope:(h + 1) * nope].astype(F32)
            o_ref[h] = (jnp.concatenate([n * invs[u] * wn, rr[:, u * rope:(u + 1) * rope]], axis=-1)
                        * scale).astype(o_ref.dtype)


def _prep_q(proj, off_qn, off_qr, cos2, sin2, wn, wr, *, hm, nope, rope, scale, seq):
    rows = proj.shape[0]
    tm = _pick(seq, 512)
    nseq = seq // tm
    body = functools.partial(_prep_q_body, hm=hm, nope=nope, rope=rope, scale=scale)
    return pl.pallas_call(
        body, grid=(rows // tm,),
        in_specs=[pl.BlockSpec((tm, hm * nope), lambda i: (i, off_qn // (hm * nope))),
                  pl.BlockSpec((tm, hm * rope), lambda i: (i, off_qr // (hm * rope))),
                  pl.BlockSpec((tm, rope), lambda i: (i % nseq, 0)),
                  pl.BlockSpec((tm, rope), lambda i: (i % nseq, 0)),
                  pl.BlockSpec((1, nope), lambda i: (0, 0)),
                  pl.BlockSpec((1, rope), lambda i: (0, 0))],
        out_specs=pl.BlockSpec((hm, tm, nope + rope), lambda i: (0, i, 0)),
        out_shape=jax.ShapeDtypeStruct((hm, rows, nope + rope), BF16),
        compiler_params=_cp("parallel"), name="mla_prep_q")(proj, proj, cos2, sin2, wn, wr)


def _prep_kv_body(ckv_ref, dk_ref, cos_ref, sin_ref, wc_ref, wk_ref, wv_ref, kn_ref, kr_ref,
                  ckv_o, kpe_o, ks_o, k_o, v_o, *, hm, nope, rope, vd, hs):
    x = ckv_ref[...].astype(F32)
    ms = jnp.mean(x * x, axis=-1, keepdims=True)
    ckv = x * lax.rsqrt(ms + EPS) * wc_ref[...]
    ckv_o[...] = ckv
    cb = ckv.astype(BF16)
    kn = _dot(cb, wk_ref[...])
    v = _dot(cb, wv_ref[...])
    kr = dk_ref[:, hs:hs + rope].astype(F32)
    krss = jnp.sum(kr * kr, axis=-1, keepdims=True)
    kpe = _rope(kr * kr_ref[...], cos_ref[...], sin_ref[...])
    kpe_o[...] = kpe
    tm = x.shape[0]
    lane = lax.broadcasted_iota(jnp.int32, (tm, hm), 1)
    ks_all = jnp.zeros((tm, hm), F32)
    for h in range(hm):
        knh = kn[:, h * nope:(h + 1) * nope]
        ss = jnp.sum(knh * knh, axis=-1, keepdims=True) + krss
        ksh = lax.rsqrt(ss / (nope + rope) + EPS)
        ks_all = jnp.where(lane == h, ksh, ks_all)
        k_o[h] = (jnp.concatenate([knh * kn_ref[...], kpe], axis=-1) * ksh).astype(k_o.dtype)
        v_o[h] = v[:, h * vd:(h + 1) * vd].astype(v_o.dtype)
    ks_o[...] = ks_all


def _prep_kv(tail, cos2, sin2, wc, wk, wv, kn_w, kr_w, *, hm, nope, rope, vd, lora, hs, seq):
    rows = tail.shape[0]
    tm = _pick(seq, 512)
    nseq = seq // tm
    body = functools.partial(_prep_kv_body, hm=hm, nope=nope, rope=rope, vd=vd, hs=hs)
    full = lambda i: (0, 0)
    return pl.pallas_call(
        body, grid=(rows // tm,),
        in_specs=[pl.BlockSpec((tm, lora), lambda i: (i, 0)),
                  pl.BlockSpec((tm, hs + rope), lambda i: (i, lora // (hs + rope))),
                  pl.BlockSpec((tm, rope), lambda i: (i % nseq, 0)),
                  pl.BlockSpec((tm, rope), lambda i: (i % nseq, 0)),
                  pl.BlockSpec((1, lora), full),
                  pl.BlockSpec((lora, hm * nope), full),
                  pl.BlockSpec((lora, hm * vd), full),
                  pl.BlockSpec((1, nope), full),
                  pl.BlockSpec((1, rope), full)],
        out_specs=[pl.BlockSpec((tm, lora), lambda i: (i, 0)),
                   pl.BlockSpec((tm, rope), lambda i: (i, 0)),
                   pl.BlockSpec((tm, hm), lambda i: (i, 0)),
                   pl.BlockSpec((hm, tm, nope + rope), lambda i: (0, i, 0)),
                   pl.BlockSpec((hm, tm, vd), lambda i: (0, i, 0))],
        out_shape=[jax.ShapeDtypeStruct((rows, lora), F32),
                   jax.ShapeDtypeStruct((rows, rope), F32),
                   jax.ShapeDtypeStruct((rows, hm), F32),
                   jax.ShapeDtypeStruct((hm, rows, nope + rope), BF16),
                   jax.ShapeDtypeStruct((hm, rows, vd), BF16)],
        compiler_params=_cp("parallel"), name="mla_prep_kv")(tail, tail, cos2, sin2, wc, wk, wv, kn_w, kr_w)


def _flash_body(q_ref, kx_ref, vx_ref, km_ref, vm_ref, o_ref, *, tq, nq, ts):
    km = km_ref[0]
    vm = vm_ref[0]

    def update(q, m, l, acc, k, v, row0=None):
        s = _dot_nt(q, k)
        if row0 is not None:
            row = lax.broadcasted_iota(jnp.int32, s.shape, 0) + row0
            col = lax.broadcasted_iota(jnp.int32, s.shape, 1)
            s = jnp.where(col <= row, s, -jnp.inf)
        m_new = jnp.maximum(m, jnp.max(s, axis=-1, keepdims=True))
        corr = jnp.exp(m - m_new)
        p = jnp.exp(s - m_new)
        return m_new, l * corr + jnp.sum(p, axis=-1, keepdims=True), acc * corr + _dot(p.astype(BF16), v)

    for i in range(nq):
        q = q_ref[0, 0, i * tq:(i + 1) * tq, :]
        s = _dot_nt(q, km)
        m = jnp.max(s, axis=-1, keepdims=True)
        p = jnp.exp(s - m)
        l = jnp.sum(p, axis=-1, keepdims=True)
        acc = _dot(p.astype(BF16), vm)
        for j in range(i):
            m, l, acc = update(q, m, l, acc, kx_ref[0, 0, j * tq:(j + 1) * tq, :], vx_ref[0, 0, j * tq:(j + 1) * tq, :])
        for r in range(tq // ts):
            rows = slice(r * ts, (r + 1) * ts)
            keys = slice(i * tq, i * tq + (r + 1) * ts)
            _, lr, ar = update(q[rows], m[rows], l[rows], acc[rows], kx_ref[0, 0, keys, :], vx_ref[0, 0, keys, :],
                               row0=r * ts)
            o_ref[i * tq + r * ts:i * tq + (r + 1) * ts, :] = (ar / lr).astype(o_ref.dtype)


def _flash(q, kx, vx, km, vm, tq_pref=2048, ts_pref=512):
    hm, batch, seq, dk = q.shape
    vd = vx.shape[-1]
    nm = km.shape[1]
    tq = _pick(seq, tq_pref)
    body = functools.partial(_flash_body, tq=tq, nq=seq // tq, ts=_pick(tq, ts_pref))
    per_bh = lambda b, h: (h, b, 0, 0)
    return pl.pallas_call(
        body, grid=(batch, hm),
        in_specs=[pl.BlockSpec((1, 1, seq, dk), per_bh),
                  pl.BlockSpec((1, 1, seq, dk), per_bh),
                  pl.BlockSpec((1, 1, seq, vd), per_bh),
                  pl.BlockSpec((1, nm, dk), lambda b, h: (h, 0, 0)),
                  pl.BlockSpec((1, nm, vd), lambda b, h: (h, 0, 0))],
        out_specs=pl.BlockSpec((seq, vd), lambda b, h: (b, h)),
        out_shape=jax.ShapeDtypeStruct((batch * seq, hm * vd), BF16),
        compiler_params=_cp("parallel", "arbitrary"), name="mla_flash")(q, kx, vx, km, vm)


def _ssd_body(z_ref, xs_ref, b_ref, c_ref, dk_ref, cwx_ref, cbx_ref, cwb_ref, cbb_ref, cwc_ref, cbc_ref,
              dtb_ref, alog_ref, dsk_ref, wn_ref, h0_ref, hx_ref, hb_ref, hc_ref,
              yn_ref, st_ref, ht_ref,
              hs, bufx, bufb, bufc, cs_s, rt_s, *, L, P, N, gs, e, lead, nch, ng):
    c = pl.program_id(1)
    gi = pl.program_id(2)
    nh = gs * e

    @pl.when(c == 0)
    def _():
        hs[gi] = h0_ref[...]
        bufx[gi] = hx_ref[...].astype(BF16)
        bufb[gi] = hb_ref[...].astype(BF16)
        bufc[gi] = hc_ref[...].astype(BF16)

    live = lax.broadcasted_iota(jnp.int32, (L, 1), 0) >= lead

    @pl.when(gi == 0)
    def _():
        x = dk_ref[...].astype(F32) + dtb_ref[...]
        dt = jnp.maximum(x, 0.0) + jnp.log(1.0 + jnp.exp(-jnp.abs(x)))
        if lead:
            dt = jnp.where(live, dt, 0.0)
        da = dt * (-jnp.exp(alog_ref[...]))
        r = lax.broadcasted_iota(jnp.int32, (L, L), 0)
        s = lax.broadcasted_iota(jnp.int32, (L, L), 1)
        tri = (r >= s).astype(BF16)
        hi, mid, lo = _split3(da)
        cs2 = (_dot(tri, hi) + _dot(tri, mid) + _dot(tri, lo)) * LOG2E
        rt_s[...] = (cs2 - jnp.log(dt) * LOG2E).T
        for k in range(ng):
            cs_s[k] = pltpu.roll(cs2, (LANES - nh * k) % LANES, axis=1) if k else cs2

    t_i = lax.broadcasted_iota(jnp.int32, (L, L + HALO), 0)
    u_i = lax.broadcasted_iota(jnp.int32, (L, L + HALO), 1)
    shifts = jnp.concatenate([(u_i == t_i + (HALO - j)) for j in (1, 2, 3)], axis=0).astype(BF16)

    par = lax.rem(c, 2)
    rd = par * ng + gi
    wr = (1 - par) * ng + gi

    def conv(buf, raw_ref, cw_ref, cb_ref):
        raw = raw_ref[...].astype(BF16)
        taps = _dot(shifts, jnp.concatenate([buf[rd], raw], axis=0))
        out = cb_ref[...] + cw_ref[3:4, :] * raw.astype(F32) + cw_ref[2:3, :] * taps[0:L] \
            + cw_ref[1:2, :] * taps[L:2 * L] + cw_ref[0:1, :] * taps[2 * L:3 * L]
        buf[wr] = raw[L - HALO:L, :]
        out = _silu(out)
        if lead:
            out = jnp.where(live, out, 0.0)
        return out

    xs = conv(bufx, xs_ref, cwx_ref, cbx_ref)
    bm = conv(bufb, b_ref, cwb_ref, cbb_ref)
    cm = conv(bufc, c_ref, cwc_ref, cbc_ref)

    cs2 = cs_s[gi]
    ecs = jnp.exp2(cs2)
    rt = rt_s[pl.ds(pl.multiple_of(gi * nh, SUBLANES), nh), :]
    r = lax.broadcasted_iota(jnp.int32, (L, L), 0)
    s = lax.broadcasted_iota(jnp.int32, (L, L), 1)
    causal = r >= s
    lo_half = lax.broadcasted_iota(jnp.int32, (1, LANES), 1) < P
    hold_all = hs[gi]

    def head(j, cbm, cg):
        ew = jnp.exp2(jnp.minimum(cs2[:, j:j + 1] - rt[j:j + 1, :], EXP2_CLAMP))
        mh = (cbm * ew).astype(BF16)
        ch = (cg * ecs[:, j:j + 1]).astype(BF16)
        return mh, ch, ew[L - 1:L, :], ecs[L - 1:L, j:j + 1]

    ys, hnew = [], []
    for gl in range(gs):
        bg = bm[:, gl * N:(gl + 1) * N]
        cg = cm[:, gl * N:(gl + 1) * N]
        cbm = jnp.where(causal, _dot_nt(cg.astype(BF16), bg.astype(BF16)), 0.0)
        bt = bg.T
        for pr in range(e // 2):
            j0 = gl * e + 2 * pr
            lane0 = (gl * (e // 2) + pr) * LANES
            m0, c0, w0, d0 = head(j0, cbm, cg)
            m1, c1, w1, d1 = head(j0 + 1, cbm, cg)
            xp = xs[:, lane0:lane0 + LANES]
            hold = hold_all[:, lane0:lane0 + LANES]
            x_lo = jnp.where(lo_half, xp, 0.0).astype(BF16)
            x_hi = jnp.where(lo_half, 0.0, xp).astype(BF16)
            h_lo = jnp.where(lo_half, hold, 0.0).astype(BF16)
            h_hi = jnp.where(lo_half, 0.0, hold).astype(BF16)
            ys.append(_dot(jnp.concatenate([m0, c0, m1, c1], axis=1),
                           jnp.concatenate([x_lo, h_lo, x_hi, h_hi], axis=0)))
            a = jnp.concatenate([(bt * w0).astype(BF16), (bt * w1).astype(BF16)], axis=1)
            upd = _dot(a, jnp.concatenate([x_lo, x_hi], axis=0))
            hnew.append(hold * jnp.where(lo_half, d0, d1) + upd)
    h_all = jnp.concatenate(hnew, axis=1)
    hs[gi] = h_all

    y = jnp.concatenate(ys, axis=1)
    y = (y + dsk_ref[...] * xs) * _silu(z_ref[...].astype(F32))
    gw = e * P
    outs = []
    for gl in range(gs):
        yg = y[:, gl * gw:(gl + 1) * gw]
        ms = jnp.mean(yg * yg, axis=-1, keepdims=True)
        outs.append(yg * lax.rsqrt(ms + EPS))
    yn_ref[...] = (jnp.concatenate(outs, axis=1) * wn_ref[...]).astype(yn_ref.dtype)

    @pl.when(c == nch - 1)
    def _():
        st_ref[0] = h_all.T
        ht_ref[0] = h_all


def _ssd(proj, tail, off_z, off_xs, off_b, off_c, off_dt, conv_wx, conv_bx, conv_wb, conv_bb, conv_wc, conv_bc,
         dtb, alog, dsk, wn, h0t, halo_x, halo_b, halo_c, *, batch, nch, P, N, G, H, lead=0, gs=8):
    L = CHUNK
    e = H // G
    di = H * P
    gw = gs * e * P
    gn = gs * N
    ng = G // gs
    body = functools.partial(_ssd_body, L=L, P=P, N=N, gs=gs, e=e, lead=lead, nch=nch, ng=ng)
    row = lambda b, c, g: b * nch + c
    last = lambda c, g: jnp.where(c == nch - 1, g, 0)
    col0 = lambda b, c, g: (0, g)
    fixed = lambda b, c, g: (0, 0)
    in_specs = [
        pl.BlockSpec((L, gw), lambda b, c, g: (row(b, c, g), off_z // gw + g)),
        pl.BlockSpec((L, gw), lambda b, c, g: (row(b, c, g), off_xs // gw + g)),
        pl.BlockSpec((L, gn), lambda b, c, g: (row(b, c, g), off_b // gn + g)),
        pl.BlockSpec((L, gn), lambda b, c, g: (row(b, c, g), off_c // gn + g)),
        pl.BlockSpec((L, LANES), lambda b, c, g: (row(b, c, g), off_dt // LANES)),
        pl.BlockSpec((4, gw), col0), pl.BlockSpec((1, gw), col0),
        pl.BlockSpec((4, gn), col0), pl.BlockSpec((1, gn), col0),
        pl.BlockSpec((4, gn), col0), pl.BlockSpec((1, gn), col0),
        pl.BlockSpec((1, LANES), fixed), pl.BlockSpec((1, LANES), fixed),
        pl.BlockSpec((1, gw), col0), pl.BlockSpec((1, gw), col0),
        pl.BlockSpec((N, gw), col0),
        pl.BlockSpec((HALO, gw), col0), pl.BlockSpec((HALO, gn), col0), pl.BlockSpec((HALO, gn), col0),
    ]
    out_specs = [
        pl.BlockSpec((L, gw), lambda b, c, g: (row(b, c, g), g)),
        pl.BlockSpec((1, gw, N), lambda b, c, g: (b, last(c, g), 0)),
        pl.BlockSpec((1, N, gw), lambda b, c, g: (b, 0, last(c, g))),
    ]
    out_shape = [jax.ShapeDtypeStruct((batch * nch * L, di), BF16),
                 jax.ShapeDtypeStruct((batch, di, N), F32),
                 jax.ShapeDtypeStruct((batch, N, di), F32)]
    scratch = [pltpu.VMEM((ng, N, gw), F32),
               pltpu.VMEM((2 * ng, HALO, gw), BF16),
               pltpu.VMEM((2 * ng, HALO, gn), BF16),
               pltpu.VMEM((2 * ng, HALO, gn), BF16),
               pltpu.VMEM((ng, L, LANES), F32),
               pltpu.VMEM((LANES, L), F32)]
    return pl.pallas_call(
        body, grid=(batch, nch, ng), in_specs=in_specs, out_specs=out_specs, out_shape=out_shape,
        scratch_shapes=scratch, compiler_params=_cp("arbitrary", "arbitrary", "arbitrary"), name="ssd_scan")(
            proj, proj, proj, proj, tail, conv_wx, conv_bx, conv_wb, conv_bb, conv_wc, conv_bc,
            dtb, alog, dsk, wn, h0t, halo_x, halo_b, halo_c)


def _merge_body(att_ref, yn_ref, wa_ref, ws_ref, ga_ref, gs_ref, o_ref):
    a = _dot(att_ref[...], wa_ref[...])
    s = _dot(yn_ref[...], ws_ref[...])
    ga = _sigmoid(ga_ref[...].astype(F32))
    gs = _sigmoid(gs_ref[...].astype(F32))
    o_ref[...] = (ga * a + gs * s).astype(o_ref.dtype)


def _merge(att, yn, wa, ws, proj, off_ga, off_gs):
    m, ka = att.shape
    ks = yn.shape[1]
    n = wa.shape[1]
    tm = _pick(m, 1024)
    tn = _pick(n, 512, LANES)
    return pl.pallas_call(
        _merge_body, grid=(m // tm, n // tn),
        in_specs=[pl.BlockSpec((tm, ka), lambda i, j: (i, 0)),
                  pl.BlockSpec((tm, ks), lambda i, j: (i, 0)),
                  pl.BlockSpec((ka, tn), lambda i, j: (0, j)),
                  pl.BlockSpec((ks, tn), lambda i, j: (0, j)),
                  pl.BlockSpec((tm, tn), lambda i, j: (i, off_ga // tn + j)),
                  pl.BlockSpec((tm, tn), lambda i, j: (i, off_gs // tn + j))],
        out_specs=pl.BlockSpec((tm, tn), lambda i, j: (i, j)),
        out_shape=jax.ShapeDtypeStruct((m, n), BF16),
        compiler_params=_cp("parallel", "arbitrary"), name="gated_merge")(att, yn, wa, ws, proj, proj)


def _outproj_body(u_ref, wo_ref, h_ref, wn_ref, h2_ref, hn_ref):
    h2 = h_ref[...].astype(F32) + _dot(u_ref[...], wo_ref[...])
    h2_ref[...] = h2
    ms = jnp.mean(h2 * h2, axis=-1, keepdims=True)
    hn_ref[...] = (h2 * lax.rsqrt(ms + EPS) * wn_ref[...]).astype(hn_ref.dtype)


def _outproj(u, wo, h, wn):
    m, d = h.shape
    tm = _pick(m, 512)
    return pl.pallas_call(
        _outproj_body, grid=(m // tm,),
        in_specs=[pl.BlockSpec((tm, d), lambda i: (i, 0)),
                  pl.BlockSpec((d, d), lambda i: (0, 0)),
                  pl.BlockSpec((tm, d), lambda i: (i, 0)),
                  pl.BlockSpec((1, d), lambda i: (0, 0))],
        out_specs=[pl.BlockSpec((tm, d), lambda i: (i, 0)), pl.BlockSpec((tm, d), lambda i: (i, 0))],
        out_shape=[jax.ShapeDtypeStruct((m, d), F32), jax.ShapeDtypeStruct((m, d), BF16)],
        compiler_params=_cp("parallel"), name="out_proj")(u, wo, h, wn.reshape(1, d))


def _ffn_in_body(x_ref, wg_ref, wu_ref, o_ref):
    x = x_ref[...]
    o_ref[...] = (_silu(_dot(x, wg_ref[...])) * _dot(x, wu_ref[...])).astype(o_ref.dtype)


def _ffn_in(x, w, f):
    m, d = x.shape
    tm = _pick(m, 2048)
    tn = _pick(f, 512, LANES)
    nf = f // tn
    return pl.pallas_call(
        _ffn_in_body, grid=(m // tm, nf),
        in_specs=[pl.BlockSpec((tm, d), lambda i, j: (i, 0)),
                  pl.BlockSpec((d, tn), lambda i, j: (0, j)),
                  pl.BlockSpec((d, tn), lambda i, j: (0, nf + j))],
        out_specs=pl.BlockSpec((tm, tn), lambda i, j: (i, j)),
        out_shape=jax.ShapeDtypeStruct((m, f), BF16),
        compiler_params=_cp("parallel", "arbitrary"), name="ffn_in")(x, w, w)


def _ffn_out_body(a_ref, wd_ref, h_ref, o_ref):
    o_ref[...] = h_ref[...] + _dot(a_ref[...], wd_ref[...])


def _ffn_out(a, wd, h):
    m, f = a.shape
    d = wd.shape[1]
    tm = _pick(m, 1024)
    tn = _pick(d, 512, LANES)
    return pl.pallas_call(
        _ffn_out_body, grid=(m // tm, d // tn),
        in_specs=[pl.BlockSpec((tm, f), lambda i, j: (i, 0)),
                  pl.BlockSpec((f, tn), lambda i, j: (0, j)),
                  pl.BlockSpec((tm, tn), lambda i, j: (i, j))],
        out_specs=pl.BlockSpec((tm, tn), lambda i, j: (i, j)),
        out_shape=jax.ShapeDtypeStruct((m, d), F32),
        compiler_params=_cp("parallel", "arbitrary"), name="ffn_out")(a, wd, h)


def _spre_body(x_ref, s0_ref, s1_ref, s2_ref, cw_ref, cb_ref, xc_ref, xt_ref):
    x = x_ref[...].astype(F32)
    out = cb_ref[...] + cw_ref[0:1, :] * s0_ref[...] + cw_ref[1:2, :] * s1_ref[...] \
        + cw_ref[2:3, :] * s2_ref[...] + cw_ref[3:4, :] * x
    xc = _silu(out)
    xc_ref[...] = xc
    xt_ref[...] = xc.T.astype(xt_ref.dtype)


def _spre(proj, off_xbc, sconv, conv_w, conv_b, *, db, cc):
    tn = _pick(cc, 2048, LANES)
    nb = cc // tn
    return pl.pallas_call(
        _spre_body, grid=(nb,),
        in_specs=[pl.BlockSpec((db, tn), lambda j: (0, off_xbc // tn + j)),
                  pl.BlockSpec((db, tn), lambda j: (0, j)),
                  pl.BlockSpec((db, tn), lambda j: (0, nb + j)),
                  pl.BlockSpec((db, tn), lambda j: (0, 2 * nb + j)),
                  pl.BlockSpec((4, tn), lambda j: (0, j)),
                  pl.BlockSpec((1, tn), lambda j: (0, j))],
        out_specs=[pl.BlockSpec((db, tn), lambda j: (0, j)), pl.BlockSpec((tn, db), lambda j: (j, 0))],
        out_shape=[jax.ShapeDtypeStruct((db, cc), F32), jax.ShapeDtypeStruct((cc, db), BF16)],
        compiler_params=_cp("parallel"), name="sample_conv")(proj, sconv, sconv, sconv, conv_w, conv_b)


def _sdt_body(dk_ref, dtb_ref, alog_ref, dt_ref, dec_ref):
    x = dk_ref[...].astype(F32) + dtb_ref[...]
    dt = jnp.maximum(x, 0.0) + jnp.log(1.0 + jnp.exp(-jnp.abs(x)))
    dt_ref[...] = dt
    dec_ref[...] = jnp.exp(dt * (-jnp.exp(alog_ref[...])))


def _sdt(tail, off_dt, dtb, alog, *, db):
    blk = pl.BlockSpec((db, LANES), lambda i: (0, 0))
    return pl.pallas_call(
        _sdt_body, grid=(1,),
        in_specs=[pl.BlockSpec((db, LANES), lambda i: (0, off_dt // LANES)),
                  pl.BlockSpec((1, LANES), lambda i: (0, 0)), pl.BlockSpec((1, LANES), lambda i: (0, 0))],
        out_specs=[blk, blk],
        out_shape=[jax.ShapeDtypeStruct((db, LANES), F32), jax.ShapeDtypeStruct((db, LANES), F32)],
        compiler_params=_cp("arbitrary"), name="sample_dt")(tail, dtb, alog)


def _sstate_body(dec_ref, dts_ref, st_ref, xt_ref, b_ref, c_ref, sto_ref, yo_ref, *, H, P, N, G, db):
    b = pl.program_id(0)
    e = H // G
    di = H * P
    bb = b_ref[0]
    h0 = st_ref[0].reshape(di, N)
    y = _dot_nt(c_ref[0].astype(BF16), h0.astype(BF16))
    grow = lax.broadcasted_iota(jnp.int32, (G, di), 0)
    gcol = lax.broadcasted_iota(jnp.int32, (G, di), 1) // (e * P)
    yo_ref[0] = jnp.sum(jnp.where(grow == gcol, y, 0.0), axis=0, keepdims=True)
    onehot = lax.broadcasted_iota(jnp.int32, (db, N), 0) == b
    for g in range(G):
        mg = jnp.where(onehot, bb[g:g + 1, :], 0.0).astype(BF16)
        u = _dot(xt_ref[g * e * P:(g + 1) * e * P, :], mg)
        for hh in range(e):
            h = g * e + hh
            sto_ref[0, h] = st_ref[0, h] * dec_ref[b, h] + dts_ref[b, h] * u[hh * P:(hh + 1) * P, :]


def _sstate(dec, dts, state, xt, bm, cm, *, H, P, N, G):
    db = state.shape[0]
    di = H * P
    body = functools.partial(_sstate_body, H=H, P=P, N=N, G=G, db=db)
    smem = pl.BlockSpec(memory_space=pltpu.SMEM)
    return pl.pallas_call(
        body, grid=(db,),
        in_specs=[smem, smem,
                  pl.BlockSpec((1, H, P, N), lambda b: (b, 0, 0, 0)),
                  pl.BlockSpec((di, db), lambda b: (0, 0)),
                  pl.BlockSpec((1, G, N), lambda b: (b, 0, 0)),
                  pl.BlockSpec((1, G, N), lambda b: (b, 0, 0))],
        out_specs=[pl.BlockSpec((1, H, P, N), lambda b: (b, 0, 0, 0)),
                   pl.BlockSpec((1, 1, di), lambda b: (b, 0, 0))],
        out_shape=[jax.ShapeDtypeStruct((db, H, P, N), F32), jax.ShapeDtypeStruct((db, 1, di), F32)],
        compiler_params=_cp("parallel"), name="sample_state")(dec, dts, state, xt, bm, cm)


def _spost_body(xs_ref, b_ref, c_ref, yo_ref, dt_ref, dec_ref, z_ref, dsk_ref, wn_ref, o_ref, *, H, P, N, G):
    e = H // G
    di = H * P
    xs = xs_ref[...]
    hrow = lax.broadcasted_iota(jnp.int32, (LANES, di), 0)
    hcol = lax.broadcasted_iota(jnp.int32, (LANES, di), 1) // P
    expand = (hrow == hcol).astype(BF16)

    def rep(v):
        hi, mid, lo = _split3(v)
        return _dot(hi, expand) + _dot(mid, expand) + _dot(lo, expand)

    dt_rep = rep(dt_ref[...])
    dec_rep = rep(dec_ref[...])
    prod = b_ref[...] * c_ref[...]
    gw = e * P
    outs = []
    for g in range(G):
        cbg = jnp.sum(prod[:, g * N:(g + 1) * N], axis=-1, keepdims=True)
        sl = slice(g * gw, (g + 1) * gw)
        y = cbg * dt_rep[:, sl] * xs[:, sl] + dec_rep[:, sl] * yo_ref[:, sl] + dsk_ref[:, sl] * xs[:, sl]
        y = y * _silu(z_ref[:, sl].astype(F32))
        ms = jnp.mean(y * y, axis=-1, keepdims=True)
        outs.append(y * lax.rsqrt(ms + EPS))
    o_ref[...] = (jnp.concatenate(outs, axis=1) * wn_ref[...]).astype(o_ref.dtype)


def _spost(xc, yoff, dt, dec, proj, off_z, dsk, wn, *, H, P, N, G):
    db = xc.shape[0]
    di = H * P
    gn = G * N
    body = functools.partial(_spost_body, H=H, P=P, N=N, G=G)
    one = lambda i: (0, 0)
    return pl.pallas_call(
        body, grid=(1,),
        in_specs=[pl.BlockSpec((db, di), one),
                  pl.BlockSpec((db, gn), lambda i: (0, di // gn)),
                  pl.BlockSpec((db, gn), lambda i: (0, di // gn + 1)),
                  pl.BlockSpec((db, di), one),
                  pl.BlockSpec((db, LANES), one), pl.BlockSpec((db, LANES), one),
                  pl.BlockSpec((db, di), lambda i: (0, off_z // di)),
                  pl.BlockSpec((1, di), one), pl.BlockSpec((1, di), one)],
        out_specs=pl.BlockSpec((db, di), one),
        out_shape=jax.ShapeDtypeStruct((db, di), BF16),
        compiler_params=_cp("arbitrary"), name="sample_ssm_post")(xc, xc, xc, yoff, dt, dec, proj, dsk, wn)


def _sq_body(qn_ref, qr_ref, cos_ref, sin_ref, wn_ref, wr_ref, kn_ref, wk_ref, qa_ref, qo_ref, *, hm, nope, rope, scale):
    cos2 = cos_ref[...]
    sin2 = sin_ref[...]
    for h in range(hm):
        n = qn_ref[:, h * nope:(h + 1) * nope].astype(F32)
        r = qr_ref[:, h * rope:(h + 1) * rope].astype(F32)
        ms = (jnp.sum(n * n, axis=-1, keepdims=True) + jnp.sum(r * r, axis=-1, keepdims=True)) / (nope + rope)
        inv = lax.rsqrt(ms + EPS)
        rr = _rope(r * inv * wr_ref[...], cos2, sin2)
        qn = (n * inv * wn_ref[...] * kn_ref[...]).astype(BF16)
        qa_ref[h] = (_dot_nt(qn, wk_ref[h]) * scale).astype(qa_ref.dtype)
        qo_ref[h] = (rr * scale).astype(qo_ref.dtype)


def _sq(proj, off_qn, off_qr, cos2, sin2, wn, wr, kn_w, wk_h, *, hm, nope, rope, lora, scale, db):
    body = functools.partial(_sq_body, hm=hm, nope=nope, rope=rope, scale=scale)
    one = lambda i: (0, 0)
    return pl.pallas_call(
        body, grid=(1,),
        in_specs=[pl.BlockSpec((db, hm * nope), lambda i: (0, off_qn // (hm * nope))),
                  pl.BlockSpec((db, hm * rope), lambda i: (0, off_qr // (hm * rope))),
                  pl.BlockSpec((db, rope), one), pl.BlockSpec((db, rope), one),
                  pl.BlockSpec((1, nope), one), pl.BlockSpec((1, rope), one), pl.BlockSpec((1, nope), one),
                  pl.BlockSpec((hm, lora, nope), lambda i: (0, 0, 0))],
        out_specs=[pl.BlockSpec((hm, db, lora), lambda i: (0, 0, 0)), pl.BlockSpec((hm, db, rope), lambda i: (0, 0, 0))],
        out_shape=[jax.ShapeDtypeStruct((hm, db, lora), BF16), jax.ShapeDtypeStruct((hm, db, rope), BF16)],
        compiler_params=_cp("arbitrary"), name="sample_q_absorb")(proj, proj, cos2, sin2, wn, wr, kn_w, wk_h)


def _decode_body(pt_ref, qa_ref, qr_ref, cn_ref, kn_ref, sn_ref, ckv_hbm, kpe_hbm, ks_hbm, o_ref,
                 m_s, l_s, acc_s, cst, kst, sst, sems, *, G, PS, U):
    bu = pl.program_id(0)
    j = pl.program_id(1)
    nj = pl.num_programs(1)
    step = bu * nj + j
    slot = lax.rem(step, 2)

    def page_copies(bb, jj, sl):
        copies = []
        for u in range(U):
            for g in range(G):
                pg = pt_ref[bb * U + u, jj * G + g]
                keys = pl.ds(g * PS, PS)
                copies.append(pltpu.make_async_copy(ckv_hbm.at[0, pg], cst.at[sl, u, keys, :], sems.at[sl, 0]))
                copies.append(pltpu.make_async_copy(kpe_hbm.at[0, pg], kst.at[sl, u, :, keys], sems.at[sl, 1]))
                copies.append(pltpu.make_async_copy(ks_hbm.at[0, pg], sst.at[sl, u, :, keys], sems.at[sl, 2]))
        return copies

    @pl.when(step == 0)
    def _():
        for cp in page_copies(bu, j, slot):
            cp.start()

    @pl.when(step + 1 < pl.num_programs(0) * nj)
    def _():
        wrap = j == nj - 1
        for cp in page_copies(jnp.where(wrap, bu + 1, bu), jnp.where(wrap, 0, j + 1), 1 - slot):
            cp.start()

    for cp in page_copies(bu, j, slot):
        cp.wait()

    @pl.when(j == 0)
    def _():
        m_s[...] = jnp.full(m_s.shape, -jnp.inf, F32)
        l_s[...] = jnp.zeros(l_s.shape, F32)
        acc_s[...] = jnp.zeros(acc_s.shape, F32)

    for u in range(U):
        cb = cst[slot, u].astype(BF16)
        qa = qa_ref[u]
        qr = qr_ref[u]
        sc = (_dot_nt(qa, cb) + _dot(qr, kst[slot, u].astype(BF16))) * sst[slot, u]
        m_old = m_s[u, :, 0:1]
        m_new = jnp.maximum(m_old, jnp.max(sc, axis=1, keepdims=True))
        corr = jnp.exp(m_old - m_new)
        p = jnp.exp(sc - m_new)
        l_new = l_s[u, :, 0:1] * corr + jnp.sum(p, axis=1, keepdims=True)
        acc = acc_s[u] * corr + _dot(p.astype(BF16), cb)
        m_s[u] = jnp.broadcast_to(m_new, m_s.shape[1:])
        l_s[u] = jnp.broadcast_to(l_new, l_s.shape[1:])
        acc_s[u] = acc

    @pl.when(j == nj - 1)
    def _():
        for u in range(U):
            cn = cn_ref[u].astype(F32)
            kn = kn_ref[u].astype(F32)
            scn = (jnp.sum(qa_ref[u].astype(F32) * cn, axis=1, keepdims=True)
                   + jnp.sum(qr_ref[u].astype(F32) * kn, axis=1, keepdims=True)) * sn_ref[u]
            m_new = m_s[u, :, 0:1]
            m_f = jnp.maximum(m_new, scn)
            corr2 = jnp.exp(m_new - m_f)
            pn = jnp.exp(scn - m_f)
            l_f = l_s[u, :, 0:1] * corr2 + pn
            o_ref[u] = (acc_s[u] * corr2 + pn * cn) / l_f


def _decode(page_table, qa, qr, ckv_n, kpe_n, ks_n, cache_ckv, cache_kpe_t, cache_ks_t, *, G, U):
    db, hm, lora = qa.shape
    rope = qr.shape[-1]
    ps = cache_ckv.shape[2]
    npg = page_table.shape[1]
    G = _pick(npg, G, 1)
    U = _pick(db, U, 1)
    body = functools.partial(_decode_body, G=G, PS=ps, U=U)
    per_b = lambda b, j, pt: (b, 0, 0)
    hbm = pl.BlockSpec(memory_space=pl.ANY)
    grid_spec = pltpu.PrefetchScalarGridSpec(
        num_scalar_prefetch=1, grid=(db // U, npg // G),
        in_specs=[pl.BlockSpec((U, hm, lora), per_b), pl.BlockSpec((U, hm, rope), per_b),
                  pl.BlockSpec((U, 1, lora), per_b), pl.BlockSpec((U, 1, rope), per_b),
                  pl.BlockSpec((U, hm, 1), per_b), hbm, hbm, hbm],
        out_specs=pl.BlockSpec((U, hm, lora), per_b),
        scratch_shapes=[pltpu.VMEM((U, hm, LANES), F32), pltpu.VMEM((U, hm, LANES), F32),
                        pltpu.VMEM((U, hm, lora), F32),
                        pltpu.VMEM((2, U, G * ps, lora), F32), pltpu.VMEM((2, U, rope, G * ps), F32),
                        pltpu.VMEM((2, U, hm, G * ps), F32), pltpu.SemaphoreType.DMA((2, 3))])
    return pl.pallas_call(
        body, grid_spec=grid_spec, out_shape=jax.ShapeDtypeStruct((db, hm, lora), F32),
        compiler_params=_cp("arbitrary", "arbitrary"), name="mla_decode")(
            page_table, qa, qr, ckv_n, kpe_n, ks_n, cache_ckv, cache_kpe_t, cache_ks_t)


def _satt_body(o_ref, wv_ref, a_ref, *, hm, vd):
    for h in range(hm):
        a_ref[:, h * vd:(h + 1) * vd] = _dot(o_ref[h].astype(BF16), wv_ref[h]).astype(a_ref.dtype)


def _satt(o_t, wv_h):
    hm, db, lora = o_t.shape
    vd = wv_h.shape[-1]
    body = functools.partial(_satt_body, hm=hm, vd=vd)
    return pl.pallas_call(
        body, grid=(1,),
        in_specs=[pl.BlockSpec((hm, db, lora), lambda i: (0, 0, 0)), pl.BlockSpec((hm, lora, vd), lambda i: (0, 0, 0))],
        out_specs=pl.BlockSpec((db, hm * vd), lambda i: (0, 0)),
        out_shape=jax.ShapeDtypeStruct((db, hm * vd), BF16),
        compiler_params=_cp("arbitrary"), name="sample_v_up")(o_t, wv_h)


def _rope_tables(pos, rope):
    half = rope // 2
    inv = ROPE_THETA ** (-jnp.arange(half, dtype=jnp.float32) / half)
    ang = pos.astype(jnp.float32)[:, None] * inv[None, :]
    cos, sin = jnp.cos(ang), jnp.sin(ang)
    return jnp.concatenate([cos, cos], axis=-1), jnp.concatenate([-sin, sin], axis=-1)


def _pad_lanes(v):
    return jnp.pad(v.astype(F32), (0, LANES - v.shape[0])).reshape(1, LANES)


def kernel(x_prompt, x_sample, cache_ckv, cache_kpe, cache_kscale, page_table, state_ssm, state_conv, meta_tokens,
           w_norm_mix, w_in, conv_w, conv_b, dt_bias, a_log, d_skip, w_ssm_norm, w_ssm_out, w_kv_norm, q_norm, k_norm,
           w_kv_up, w_attn_out, w_out, w_norm_ffn, w_ffn_in, w_ffn_out):
    batch, seq, d = x_prompt.shape
    db, ts, _ = x_sample.shape
    depth = w_in.shape[0]
    assert depth == 1 and ts == 1, "single layer, one decode token per sequence"
    nm = meta_tokens.shape[0]
    H, P, N = state_ssm.shape[2:]
    di = H * P
    cc = conv_w.shape[2]
    G = (cc - di) // (2 * N)
    gn = G * N
    hm = w_kv_up.shape[2]
    qkh = q_norm.shape[1]
    rope = cache_kpe.shape[3]
    nope = qkh - rope
    vd = w_kv_up.shape[3] - nope
    lora = w_kv_up.shape[1]
    ps = cache_ckv.shape[2]
    past = page_table.shape[1] * ps
    f = w_ffn_out.shape[1]
    scale = qkh ** -0.5
    assert H + rope == LANES and seq % CHUNK == 0 and nm <= CHUNK and P * 2 == LANES

    w0 = w_in[0]
    cuts, acc = [], 0
    for c in (di, cc, H, hm * qkh, lora, rope, d, d):
        cuts.append((acc, acc + c))
        acc += c
    w_z, w_xbc, w_dt, w_q, w_ckv, w_kr, w_ga, w_gs = (w0[:, a:b] for a, b in cuts)
    w_q = w_q.reshape(d, hm, qkh)
    w_qn = w_q[:, :, :nope].reshape(d, hm * nope)
    w_qr = w_q[:, :, nope:].reshape(d, hm * rope)
    w_main = jnp.concatenate([w_z, w_xbc, w_qn, w_qr, w_ga, w_gs], axis=1).astype(BF16)
    w_tail = jnp.concatenate([w_ckv, w_dt, w_kr], axis=1).astype(BF16)
    off_z, off_xs, off_b, off_c = 0, di, 2 * di, 2 * di + gn
    off_qn = di + cc
    off_qr = off_qn + hm * nope
    off_ga = off_qr + hm * rope
    off_gs = off_ga + d
    off_dt = lora

    wkv = w_kv_up[0]
    wk_flat = wkv[:, :, :nope].reshape(lora, hm * nope).astype(BF16)
    wv_flat = wkv[:, :, nope:].reshape(lora, hm * vd).astype(BF16)
    wk_h = jnp.transpose(wkv[:, :, :nope], (1, 0, 2)).astype(BF16)
    wv_h = jnp.transpose(wkv[:, :, nope:], (1, 0, 2)).astype(BF16)
    wa = w_attn_out[0].astype(BF16)
    ws = w_ssm_out[0].astype(BF16)
    wo = w_out[0].astype(BF16)
    wf_in = w_ffn_in[0].astype(BF16)
    wf_out = w_ffn_out[0].astype(BF16)
    qn_w = q_norm[0, :nope].reshape(1, nope)
    qr_w = q_norm[0, nope:].reshape(1, rope)
    kn_w = k_norm[0, :nope].reshape(1, nope)
    kr_w = k_norm[0, nope:].reshape(1, rope)
    wc = w_kv_norm[0].reshape(1, lora)
    cw, cbias = conv_w[0], conv_b[0].reshape(1, cc)
    cwx, cwb, cwc = cw[:, :di], cw[:, di:di + gn], cw[:, di + gn:]
    cbx, cbb, cbc = cbias[:, :di], cbias[:, di:di + gn], cbias[:, di + gn:]
    dtb = _pad_lanes(dt_bias[0])
    alog = _pad_lanes(a_log[0])
    dsk = jnp.repeat(d_skip[0].astype(F32), P).reshape(1, di)
    wsn = w_ssm_norm[0].reshape(1, di)

    cos_p, sin_p = _rope_tables(jnp.arange(nm + seq), rope)
    cos_s, sin_s = _rope_tables(past + jnp.arange(ts), rope)
    small = db + nm
    cos_small = jnp.concatenate([jnp.broadcast_to(cos_s, (db, rope)), cos_p[:nm]], axis=0)
    sin_small = jnp.concatenate([jnp.broadcast_to(sin_s, (db, rope)), sin_p[:nm]], axis=0)

    xp = x_prompt.reshape(batch * seq, d)
    hn_p = _rmsnorm(xp, w_norm_mix[0])
    proj_p = _matmul(hn_p, w_main, BF16, tm_pref=2048, name="in_proj")
    tail_p = _matmul(hn_p, w_tail, F32, name="in_proj_tail")
    xsm = jnp.concatenate([x_sample.reshape(db, d), meta_tokens.astype(x_prompt.dtype)], axis=0)
    hn_s = _rmsnorm(xsm, w_norm_mix[0])
    proj_s = _matmul(hn_s, w_main, F32, name="in_proj_small")
    tail_s = _matmul(hn_s, w_tail, F32, name="in_proj_tail_small")

    lead = CHUNK - nm
    pad = lambda a: jnp.pad(a, ((lead, 0), (0, 0)))
    proj_m, tail_m = pad(proj_s[db:]), pad(tail_s[db:])
    zeros_h = jnp.zeros((N, di), F32)
    zx, zb = jnp.zeros((HALO, di), F32), jnp.zeros((HALO, gn), F32)
    ssd_w = (cwx, cbx, cwb, cbb, cwc, cbc, dtb, alog, dsk, wsn)
    _, _, ht_meta = _ssd(proj_m, tail_m, off_z, off_xs, off_b, off_c, off_dt, *ssd_w, zeros_h, zx, zb, zb,
                         batch=1, nch=1, P=P, N=N, G=G, H=H, lead=lead)
    halo = proj_m[CHUNK - HALO:]
    yn_p, st_p, _ = _ssd(proj_p, tail_p, off_z, off_xs, off_b, off_c, off_dt, *ssd_w, ht_meta[0],
                         halo[:, off_xs:off_xs + di], halo[:, off_b:off_b + gn], halo[:, off_c:off_c + gn],
                         batch=batch, nch=seq // CHUNK, P=P, N=N, G=G, H=H)

    q_p = _prep_q(proj_p, off_qn, off_qr, cos_p[nm:], sin_p[nm:], qn_w, qr_w,
                  hm=hm, nope=nope, rope=rope, scale=scale, seq=seq)
    mla = dict(hm=hm, nope=nope, rope=rope, vd=vd, lora=lora, hs=H)
    ckv_x, kpe_x, ks_x, k_x, v_x = _prep_kv(tail_p, cos_p[nm:], sin_p[nm:], wc, wk_flat, wv_flat, kn_w, kr_w,
                                            seq=seq, **mla)
    ckv_sm, kpe_sm, ks_sm, k_sm, v_sm = _prep_kv(tail_s, cos_small, sin_small, wc, wk_flat, wv_flat, kn_w, kr_w,
                                                 seq=small, **mla)
    att_p = _flash(q_p.reshape(hm, batch, seq, qkh), k_x.reshape(hm, batch, seq, qkh),
                   v_x.reshape(hm, batch, seq, vd), k_sm[:, db:], v_sm[:, db:])

    u_p = _merge(att_p, yn_p, wa, ws, proj_p, off_ga, off_gs)
    h2_p, hn2_p = _outproj(u_p, wo, xp, w_norm_ffn[0])
    y_p = _ffn_out(_ffn_in(hn2_p, wf_in, f), wf_out, h2_p)

    sconv = state_conv[0].reshape(db, 3 * cc)
    xc_s, xt_s = _spre(proj_s, off_xs, sconv, cw, cbias, db=db, cc=cc)
    dt_s, dec_s = _sdt(tail_s, off_dt, dtb, alog, db=db)
    bm_s = xc_s[:, di:di + gn].reshape(db, G, N)
    cm_s = xc_s[:, di + gn:].reshape(db, G, N)
    st_s, yoff_s = _sstate(dec_s[:, :H], dt_s[:, :H], state_ssm[0], xt_s[:di], bm_s, cm_s, H=H, P=P, N=N, G=G)
    yn_s = _spost(xc_s, yoff_s.reshape(db, di), dt_s, dec_s, proj_s, off_z, dsk, wsn, H=H, P=P, N=N, G=G)

    qa_h, qr_h = _sq(proj_s, off_qn, off_qr, cos_small, sin_small, qn_w, qr_w, kn_w, wk_h,
                     hm=hm, nope=nope, rope=rope, lora=lora, scale=scale, db=db)
    qa_b = jnp.transpose(qa_h, (1, 0, 2))
    qr_b = jnp.transpose(qr_h, (1, 0, 2))
    o_s = _decode(page_table, qa_b, qr_b, ckv_sm[:db].reshape(db, 1, lora), kpe_sm[:db].reshape(db, 1, rope),
                  ks_sm[:db].reshape(db, hm, 1), cache_ckv, jnp.swapaxes(cache_kpe, 2, 3),
                  jnp.swapaxes(cache_kscale, 2, 3), G=64, U=1)
    att_s = _satt(jnp.transpose(o_s, (1, 0, 2)), wv_h)

    u_s = _merge(att_s, yn_s, wa, ws, proj_s, off_ga, off_gs)
    h2_s, hn2_s = _outproj(u_s, wo, x_sample.reshape(db, d), w_norm_ffn[0])
    y_s = _ffn_out(_ffn_in(hn2_s, wf_in, f), wf_out, h2_s)

    def with_meta(xpart, mpart):
        c = xpart.shape[-1]
        m = jnp.broadcast_to(mpart[None], (batch, nm, c))
        return jnp.concatenate([m, xpart.reshape(batch, seq, c)], axis=1)[None]

    conv_p = proj_p.reshape(batch, seq, -1)[:, seq - 3:, off_xs:off_xs + cc].astype(F32)[None]
    conv_s = jnp.concatenate([state_conv[0][:, 1:], proj_s[:db, None, off_xs:off_xs + cc]], axis=1)[None]
    return (y_p.reshape(batch, seq, d), y_s.reshape(db, ts, d),
            with_meta(ckv_x, ckv_sm[db:]), with_meta(kpe_x, kpe_sm[db:]), with_meta(ks_x, ks_sm[db:]),
            st_p.reshape(batch, H, P, N)[None], conv_p,
            ckv_sm[:db].reshape(db, ts, lora)[None], kpe_sm[:db].reshape(db, ts, rope)[None],
            ks_sm[:db].reshape(db, ts, hm)[None], st_s[None], conv_s)
```

```python
import functools

import jax
import jax.numpy as jnp
from jax import lax
from jax.experimental import pallas as pl
from jax.experimental.pallas import tpu as pltpu

EPS = 1e-6
ROPE_THETA = 10000.0
CHUNK = 128
VMEM_LIMIT_BYTES = 56 * 1024 * 1024
LANES = 128
SUBLANES = 8
HALO = 16
LOG2E = 1.4426950408889634
EXP2_CLAMP = 80.0

F32 = jnp.float32
BF16 = jnp.bfloat16


def _cp(*sem):
    return pltpu.CompilerParams(dimension_semantics=sem, vmem_limit_bytes=VMEM_LIMIT_BYTES)


def _pick(n, pref, align=SUBLANES):
    if n <= pref:
        return n
    for t in range(pref - pref % align, 0, -align):
        if n % t == 0:
            return t
    return n


def _sigmoid(x):
    return 1.0 / (1.0 + jnp.exp(-x))


def _silu(x):
    h = 0.5 * x
    return h * jnp.tanh(h) + h


def _dot(a, b):
    return jnp.dot(a, b, preferred_element_type=F32)


def _dot_nt(a, b):
    return lax.dot_general(a, b, (((1,), (1,)), ((), ())), preferred_element_type=F32)


def _dot_tn(a, b):
    return lax.dot_general(a, b, (((0,), (0,)), ((), ())), preferred_element_type=F32)


def _split3(x):
    hi = x.astype(BF16)
    r = x - hi.astype(F32)
    mid = r.astype(BF16)
    lo = (r - mid.astype(F32)).astype(BF16)
    return hi, mid, lo


def _rope(x, cos2, sin2):
    half = x.shape[-1] // 2
    sw = jnp.concatenate([x[:, half:], x[:, :half]], axis=-1)
    return x * cos2 + sw * sin2


def _rmsnorm_body(x_ref, w_ref, o_ref):
    x = x_ref[...].astype(F32)
    ms = jnp.mean(x * x, axis=-1, keepdims=True)
    o_ref[...] = (x * lax.rsqrt(ms + EPS) * w_ref[...]).astype(o_ref.dtype)


def _rmsnorm(x, w):
    m, d = x.shape
    tm = _pick(m, 512)
    return pl.pallas_call(
        _rmsnorm_body, grid=(m // tm,),
        in_specs=[pl.BlockSpec((tm, d), lambda i: (i, 0)), pl.BlockSpec((1, d), lambda i: (0, 0))],
        out_specs=pl.BlockSpec((tm, d), lambda i: (i, 0)),
        out_shape=jax.ShapeDtypeStruct((m, d), BF16),
        compiler_params=_cp("parallel"), name="rmsnorm")(x, w.reshape(1, d))


def _mm_body(x_ref, w_ref, o_ref):
    o_ref[...] = _dot(x_ref[...], w_ref[...]).astype(o_ref.dtype)


def _matmul(x, w, out_dtype, tm_pref=1024, tn_pref=1024, name="matmul"):
    m, k = x.shape
    n = w.shape[1]
    tm = _pick(m, tm_pref)
    tn = _pick(n, tn_pref, LANES)
    return pl.pallas_call(
        _mm_body, grid=(m // tm, n // tn),
        in_specs=[pl.BlockSpec((tm, k), lambda i, j: (i, 0)), pl.BlockSpec((k, tn), lambda i, j: (0, j))],
        out_specs=pl.BlockSpec((tm, tn), lambda i, j: (i, j)),
        out_shape=jax.ShapeDtypeStruct((m, n), out_dtype),
        compiler_params=_cp("parallel", "arbitrary"), name=name)(x, w)


def _prep_q_body(qn_ref, qr_ref, cos_ref, sin_ref, wn_ref, wr_ref, o_ref, *, hm, nope, rope, scale):
    half = rope // 2
    per_vreg = LANES // rope
    r_all = qr_ref[...].astype(F32)
    wr_all = r_all.shape[1]
    lane_all = lax.broadcasted_iota(jnp.int32, (1, wr_all), 1)
    first_half = lax.rem(lane_all, rope) < half
    sw_all = jnp.where(first_half, pltpu.roll(r_all, wr_all - half, axis=1), pltpu.roll(r_all, half, axis=1))
    cos_t = jnp.concatenate([cos_ref[...]] * per_vreg, axis=1)
    sin_t = jnp.concatenate([sin_ref[...]] * per_vreg, axis=1)
    wr_t = jnp.concatenate([wr_ref[...]] * per_vreg, axis=1)
    wr_sw_t = jnp.concatenate([jnp.concatenate([wr_ref[:, half:], wr_ref[:, :half]], axis=1)] * per_vreg, axis=1)
    wn = wn_ref[...]
    rsq = r_all * r_all
    lane = lax.broadcasted_iota(jnp.int32, (1, LANES), 1)
    for c in range(hm // per_vreg):
        cols = slice(c * LANES, (c + 1) * LANES)
        heads = [c * per_vreg + u for u in range(per_vreg)]
        invs = []
        for u, h in enumerate(heads):
            n = qn_ref[:, h * nope:(h + 1) * nope].astype(F32)
            seg = rsq[:, c * LANES + u * rope:c * LANES + (u + 1) * rope]
            ms = (jnp.sum(n * n, axis=-1, keepdims=True) + jnp.sum(seg, axis=-1, keepdims=True)) / (nope + rope)
            invs.append(lax.rsqrt(ms + EPS))
        inv_t = invs[0]
        for u in range(1, per_vreg):
            inv_t = jnp.where(lane < u * rope, inv_t, invs[u])
        rr = r_all[:, cols] * inv_t * wr_t * cos_t + sw_all[:, cols] * inv_t * wr_sw_t * sin_t
        for u, h in enumerate(heads):
            n = qn_ref[:, h * nope:(h + 1) * nope].astype(F32)
            o_ref[h] = (jnp.concatenate([n * invs[u] * wn, rr[:, u * rope:(u + 1) * rope]], axis=-1)
                        * scale).astype(o_ref.dtype)


def _prep_q(proj, off_qn, off_qr, cos2, sin2, wn, wr, *, hm, nope, rope, scale, seq):
    rows = proj.shape[0]
    tm = _pick(seq, 512)
    nseq = seq // tm
    body = functools.partial(_prep_q_body, hm=hm, nope=nope, rope=rope, scale=scale)
    return pl.pallas_call(
        body, grid=(rows // tm,),
        in_specs=[pl.BlockSpec((tm, hm * nope), lambda i: (i, off_qn // (hm * nope))),
                  pl.BlockSpec((tm, hm * rope), lambda i: (i, off_qr // (hm * rope))),
                  pl.BlockSpec((tm, rope), lambda i: (i % nseq, 0)),
                  pl.BlockSpec((tm, rope), lambda i: (i % nseq, 0)),
                  pl.BlockSpec((1, nope), lambda i: (0, 0)),
                  pl.BlockSpec((1, rope), lambda i: (0, 0))],
        out_specs=pl.BlockSpec((hm, tm, nope + rope), lambda i: (0, i, 0)),
        out_shape=jax.ShapeDtypeStruct((hm, rows, nope + rope), BF16),
        compiler_params=_cp("parallel"), name="mla_prep_q")(proj, proj, cos2, sin2, wn, wr)


def _prep_kv_body(ckv_ref, dk_ref, cos_ref, sin_ref, wc_ref, wk_ref, wv_ref, kn_ref, kr_ref,
                  ckv_o, kpe_o, ks_o, k_o, v_o, *, hm, nope, rope, vd, hs):
    x = ckv_ref[...].astype(F32)
    ms = jnp.mean(x * x, axis=-1, keepdims=True)
    ckv = x * lax.rsqrt(ms + EPS) * wc_ref[...]
    ckv_o[...] = ckv
    cb = ckv.astype(BF16)
    kn = _dot(cb, wk_ref[...])
    v = _dot(cb, wv_ref[...])
    kr = dk_ref[:, hs:hs + rope].astype(F32)
    krss = jnp.sum(kr * kr, axis=-1, keepdims=True)
    kpe = _rope(kr * kr_ref[...], cos_ref[...], sin_ref[...])
    kpe_o[...] = kpe
    tm = x.shape[0]
    lane = lax.broadcasted_iota(jnp.int32, (tm, hm), 1)
    ks_all = jnp.zeros((tm, hm), F32)
    for h in range(hm):
        knh = kn[:, h * nope:(h + 1) * nope]
        ss = jnp.sum(knh * knh, axis=-1, keepdims=True) + krss
        ksh = lax.rsqrt(ss / (nope + rope) + EPS)
        ks_all = jnp.where(lane == h, ksh, ks_all)
        k_o[h] = (jnp.concatenate([knh * kn_ref[...], kpe], axis=-1) * ksh).astype(k_o.dtype)
        v_o[h] = v[:, h * vd:(h + 1) * vd].astype(v_o.dtype)
    ks_o[...] = ks_all


def _prep_kv(tail, cos2, sin2, wc, wk, wv, kn_w, kr_w, *, hm, nope, rope, vd, lora, hs, seq):
    rows = tail.shape[0]
    tm = _pick(seq, 512)
    nseq = seq // tm
    body = functools.partial(_prep_kv_body, hm=hm, nope=nope, rope=rope, vd=vd, hs=hs)
    full = lambda i: (0, 0)
    return pl.pallas_call(
        body, grid=(rows // tm,),
        in_specs=[pl.BlockSpec((tm, lora), lambda i: (i, 0)),
                  pl.BlockSpec((tm, hs + rope), lambda i: (i, lora // (hs + rope))),
                  pl.BlockSpec((tm, rope), lambda i: (i % nseq, 0)),
                  pl.BlockSpec((tm, rope), lambda i: (i % nseq, 0)),
                  pl.BlockSpec((1, lora), full),
                  pl.BlockSpec((lora, hm * nope), full),
                  pl.BlockSpec((lora, hm * vd), full),
                  pl.BlockSpec((1, nope), full),
                  pl.BlockSpec((1, rope), full)],
        out_specs=[pl.BlockSpec((tm, lora), lambda i: (i, 0)),
                   pl.BlockSpec((tm, rope), lambda i: (i, 0)),
                   pl.BlockSpec((tm, hm), lambda i: (i, 0)),
                   pl.BlockSpec((hm, tm, nope + rope), lambda i: (0, i, 0)),
                   pl.BlockSpec((hm, tm, vd), lambda i: (0, i, 0))],
        out_shape=[jax.ShapeDtypeStruct((rows, lora), F32),
                   jax.ShapeDtypeStruct((rows, rope), F32),
                   jax.ShapeDtypeStruct((rows, hm), F32),
                   jax.ShapeDtypeStruct((hm, rows, nope + rope), BF16),
                   jax.ShapeDtypeStruct((hm, rows, vd), BF16)],
        compiler_params=_cp("parallel"), name="mla_prep_kv")(tail, tail, cos2, sin2, wc, wk, wv, kn_w, kr_w)


def _flash_body(q_ref, kx_ref, vx_ref, km_ref, vm_ref, o_ref, *, tq, nq, ts):
    km = km_ref[0]
    vm = vm_ref[0]

    def update(q, m, l, acc, k, v, row0=None):
        s = _dot_nt(q, k)
        if row0 is not None:
            row = lax.broadcasted_iota(jnp.int32, s.shape, 0) + row0
            col = lax.broadcasted_iota(jnp.int32, s.shape, 1)
            s = jnp.where(col <= row, s, -jnp.inf)
        m_new = jnp.maximum(m, jnp.max(s, axis=-1, keepdims=True))
        corr = jnp.exp(m - m_new)
        p = jnp.exp(s - m_new)
        return m_new, l * corr + jnp.sum(p, axis=-1, keepdims=True), acc * corr + _dot(p.astype(BF16), v)

    for i in range(nq):
        q = q_ref[0, 0, i * tq:(i + 1) * tq, :]
        s = _dot_nt(q, km)
        m = jnp.max(s, axis=-1, keepdims=True)
        p = jnp.exp(s - m)
        l = jnp.sum(p, axis=-1, keepdims=True)
        acc = _dot(p.astype(BF16), vm)
        for j in range(i):
            m, l, acc = update(q, m, l, acc, kx_ref[0, 0, j * tq:(j + 1) * tq, :], vx_ref[0, 0, j * tq:(j + 1) * tq, :])
        for r in range(tq // ts):
            rows = slice(r * ts, (r + 1) * ts)
            keys = slice(i * tq, i * tq + (r + 1) * ts)
            _, lr, ar = update(q[rows], m[rows], l[rows], acc[rows], kx_ref[0, 0, keys, :], vx_ref[0, 0, keys, :],
                               row0=r * ts)
            o_ref[i * tq + r * ts:i * tq + (r + 1) * ts, :] = (ar / lr).astype(o_ref.dtype)


def _flash(q, kx, vx, km, vm, tq_pref=2048, ts_pref=512):
    hm, batch, seq, dk = q.shape
    vd = vx.shape[-1]
    nm = km.shape[1]
    tq = _pick(seq, tq_pref)
    body = functools.partial(_flash_body, tq=tq, nq=seq // tq, ts=_pick(tq, ts_pref))
    per_bh = lambda b, h: (h, b, 0, 0)
    return pl.pallas_call(
        body, grid=(batch, hm),
        in_specs=[pl.BlockSpec((1, 1, seq, dk), per_bh),
                  pl.BlockSpec((1, 1, seq, dk), per_bh),
                  pl.BlockSpec((1, 1, seq, vd), per_bh),
                  pl.BlockSpec((1, nm, dk), lambda b, h: (h, 0, 0)),
                  pl.BlockSpec((1, nm, vd), lambda b, h: (h, 0, 0))],
        out_specs=pl.BlockSpec((seq, vd), lambda b, h: (b, h)),
        out_shape=jax.ShapeDtypeStruct((batch * seq, hm * vd), BF16),
        compiler_params=_cp("parallel", "arbitrary"), name="mla_flash")(q, kx, vx, km, vm)


def _ssd_body(z_ref, xs_ref, b_ref, c_ref, dk_ref, cwx_ref, cbx_ref, cwb_ref, cbb_ref, cwc_ref, cbc_ref,
              dtb_ref, alog_ref, dsk_ref, wn_ref, h0_ref, hx_ref, hb_ref, hc_ref,
              yn_ref, st_ref, ht_ref,
              hs, bufx, bufb, bufc, cs_s, rt_s, *, L, P, N, gs, e, lead, nch, ng):
    c = pl.program_id(1)
    gi = pl.program_id(2)
    nh = gs * e

    @pl.when(c == 0)
    def _():
        hs[gi] = h0_ref[...]
        bufx[gi] = hx_ref[...].astype(BF16)
        bufb[gi] = hb_ref[...].astype(BF16)
        bufc[gi] = hc_ref[...].astype(BF16)

    live = lax.broadcasted_iota(jnp.int32, (L, 1), 0) >= lead

    @pl.when(gi == 0)
    def _():
        x = dk_ref[...].astype(F32) + dtb_ref[...]
        dt = jnp.maximum(x, 0.0) + jnp.log(1.0 + jnp.exp(-jnp.abs(x)))
        if lead:
            dt = jnp.where(live, dt, 0.0)
        da = dt * (-jnp.exp(alog_ref[...]))
        r = lax.broadcasted_iota(jnp.int32, (L, L), 0)
        s = lax.broadcasted_iota(jnp.int32, (L, L), 1)
        tri = (r >= s).astype(BF16)
        hi, mid, lo = _split3(da)
        cs2 = (_dot(tri, hi) + _dot(tri, mid) + _dot(tri, lo)) * LOG2E
        rt_s[...] = (cs2 - jnp.log(dt) * LOG2E).T
        for k in range(ng):
            cs_s[k] = pltpu.roll(cs2, (LANES - nh * k) % LANES, axis=1) if k else cs2

    t_i = lax.broadcasted_iota(jnp.int32, (L, L + HALO), 0)
    u_i = lax.broadcasted_iota(jnp.int32, (L, L + HALO), 1)
    shifts = jnp.concatenate([(u_i == t_i + (HALO - j)) for j in (1, 2, 3)], axis=0).astype(BF16)

    par = lax.rem(c, 2)
    rd = par * ng + gi
    wr = (1 - par) * ng + gi

    def conv(buf, raw_ref, cw_ref, cb_ref):
        raw = raw_ref[...].astype(BF16)
        taps = _dot(shifts, jnp.concatenate([buf[rd], raw], axis=0))
        out = cb_ref[...] + cw_ref[3:4, :] * raw.astype(F32) + cw_ref[2:3, :] * taps[0:L] \
            + cw_ref[1:2, :] * taps[L:2 * L] + cw_ref[0:1, :] * taps[2 * L:3 * L]
        buf[wr] = raw[L - HALO:L, :]
        out = _silu(out)
        if lead:
            out = jnp.where(live, out, 0.0)
        return out

    xs = conv(bufx, xs_ref, cwx_ref, cbx_ref)
    bm = conv(bufb, b_ref, cwb_ref, cbb_ref)
    cm = conv(bufc, c_ref, cwc_ref, cbc_ref)

    cs2 = cs_s[gi]
    ecs = jnp.exp2(cs2)
    rt = rt_s[pl.ds(pl.multiple_of(gi * nh, SUBLANES), nh), :]
    r = lax.broadcasted_iota(jnp.int32, (L, L), 0)
    s = lax.broadcasted_iota(jnp.int32, (L, L), 1)
    causal = r >= s
    lo_half = lax.broadcasted_iota(jnp.int32, (1, LANES), 1) < P
    hold_all = hs[gi]

    def head(j, cbm, cg):
        ew = jnp.exp2(jnp.minimum(cs2[:, j:j + 1] - rt[j:j + 1, :], EXP2_CLAMP))
        mh = (cbm * ew).astype(BF16)
        ch = (cg * ecs[:, j:j + 1]).astype(BF16)
        return mh, ch, ew[L - 1:L, :], ecs[L - 1:L, j:j + 1]

    ys, hnew = [], []
    for gl in range(gs):
        bg = bm[:, gl * N:(gl + 1) * N]
        cg = cm[:, gl * N:(gl + 1) * N]
        cbm = jnp.where(causal, _dot_nt(cg.astype(BF16), bg.astype(BF16)), 0.0)
        bt = bg.T
        for pr in range(e // 2):
            j0 = gl * e + 2 * pr
            lane0 = (gl * (e // 2) + pr) * LANES
            m0, c0, w0, d0 = head(j0, cbm, cg)
            m1, c1, w1, d1 = head(j0 + 1, cbm, cg)
            xp = xs[:, lane0:lane0 + LANES]
            hold = hold_all[:, lane0:lane0 + LANES]
            x_lo = jnp.where(lo_half, xp, 0.0).astype(BF16)
            x_hi = jnp.where(lo_half, 0.0, xp).astype(BF16)
            h_lo = jnp.where(lo_half, hold, 0.0).astype(BF16)
            h_hi = jnp.where(lo_half, 0.0, hold).astype(BF16)
            ys.append(_dot(jnp.concatenate([m0, c0, m1, c1], axis=1),
                           jnp.concatenate([x_lo, h_lo, x_hi, h_hi], axis=0)))
            a = jnp.concatenate([(bt * w0).astype(BF16), (bt * w1).astype(BF16)], axis=1)
            upd = _dot(a, jnp.concatenate([x_lo, x_hi], axis=0))
            hnew.append(hold * jnp.where(lo_half, d0, d1) + upd)
    h_all = jnp.concatenate(hnew, axis=1)
    hs[gi] = h_all

    y = jnp.concatenate(ys, axis=1)
    y = (y + dsk_ref[...] * xs) * _silu(z_ref[...].astype(F32))
    gw = e * P
    outs = []
    for gl in range(gs):
        yg = y[:, gl * gw:(gl + 1) * gw]
        ms = jnp.mean(yg * yg, axis=-1, keepdims=True)
        outs.append(yg * lax.rsqrt(ms + EPS))
    yn_ref[...] = (jnp.concatenate(outs, axis=1) * wn_ref[...]).astype(yn_ref.dtype)

    @pl.when(c == nch - 1)
    def _():
        st_ref[0] = h_all.T
        ht_ref[0] = h_all


def _ssd(proj, tail, off_z, off_xs, off_b, off_c, off_dt, conv_wx, conv_bx, conv_wb, conv_bb, conv_wc, conv_bc,
         dtb, alog, dsk, wn, h0t, halo_x, halo_b, halo_c, *, batch, nch, P, N, G, H, lead=0, gs=8):
    L = CHUNK
    e = H // G
    di = H * P
    gw = gs * e * P
    gn = gs * N
    ng = G // gs
    body = functools.partial(_ssd_body, L=L, P=P, N=N, gs=gs, e=e, lead=lead, nch=nch, ng=ng)
    row = lambda b, c, g: b * nch + c
    last = lambda c, g: jnp.where(c == nch - 1, g, 0)
    col0 = lambda b, c, g: (0, g)
    fixed = lambda b, c, g: (0, 0)
    in_specs = [
        pl.BlockSpec((L, gw), lambda b, c, g: (row(b, c, g), off_z // gw + g)),
        pl.BlockSpec((L, gw), lambda b, c, g: (row(b, c, g), off_xs // gw + g)),
        pl.BlockSpec((L, gn), lambda b, c, g: (row(b, c, g), off_b // gn + g)),
        pl.BlockSpec((L, gn), lambda b, c, g: (row(b, c, g), off_c // gn + g)),
        pl.BlockSpec((L, LANES), lambda b, c, g: (row(b, c, g), off_dt // LANES)),
        pl.BlockSpec((4, gw), col0), pl.BlockSpec((1, gw), col0),
        pl.BlockSpec((4, gn), col0), pl.BlockSpec((1, gn), col0),
        pl.BlockSpec((4, gn), col0), pl.BlockSpec((1, gn), col0),
        pl.BlockSpec((1, LANES), fixed), pl.BlockSpec((1, LANES), fixed),
        pl.BlockSpec((1, gw), col0), pl.BlockSpec((1, gw), col0),
        pl.BlockSpec((N, gw), col0),
        pl.BlockSpec((HALO, gw), col0), pl.BlockSpec((HALO, gn), col0), pl.BlockSpec((HALO, gn), col0),
    ]
    out_specs = [
        pl.BlockSpec((L, gw), lambda b, c, g: (row(b, c, g), g)),
        pl.BlockSpec((1, gw, N), lambda b, c, g: (b, last(c, g), 0)),
        pl.BlockSpec((1, N, gw), lambda b, c, g: (b, 0, last(c, g))),
    ]
    out_shape = [jax.ShapeDtypeStruct((batch * nch * L, di), BF16),
                 jax.ShapeDtypeStruct((batch, di, N), F32),
                 jax.ShapeDtypeStruct((batch, N, di), F32)]
    scratch = [pltpu.VMEM((ng, N, gw), F32),
               pltpu.VMEM((2 * ng, HALO, gw), BF16),
               pltpu.VMEM((2 * ng, HALO, gn), BF16),
               pltpu.VMEM((2 * ng, HALO, gn), BF16),
               pltpu.VMEM((ng, L, LANES), F32),
               pltpu.VMEM((LANES, L), F32)]
    return pl.pallas_call(
        body, grid=(batch, nch, ng), in_specs=in_specs, out_specs=out_specs, out_shape=out_shape,
        scratch_shapes=scratch, compiler_params=_cp("arbitrary", "arbitrary", "arbitrary"), name="ssd_scan")(
            proj, proj, proj, proj, tail, conv_wx, conv_bx, conv_wb, conv_bb, conv_wc, conv_bc,
            dtb, alog, dsk, wn, h0t, halo_x, halo_b, halo_c)


def _merge_body(att_ref, yn_ref, wa_ref, ws_ref, ga_ref, gs_ref, o_ref):
    a = _dot(att_ref[...], wa_ref[...])
    s = _dot(yn_ref[...], ws_ref[...])
    ga = _sigmoid(ga_ref[...].astype(F32))
    gs = _sigmoid(gs_ref[...].astype(F32))
    o_ref[...] = (ga * a + gs * s).astype(o_ref.dtype)


def _merge(att, yn, wa, ws, proj, off_ga, off_gs):
    m, ka = att.shape
    ks = yn.shape[1]
    n = wa.shape[1]
    tm = _pick(m, 1024)
    tn = _pick(n, 512, LANES)
    return pl.pallas_call(
        _merge_body, grid=(m // tm, n // tn),
        in_specs=[pl.BlockSpec((tm, ka), lambda i, j: (i, 0)),
                  pl.BlockSpec((tm, ks), lambda i, j: (i, 0)),
                  pl.BlockSpec((ka, tn), lambda i, j: (0, j)),
                  pl.BlockSpec((ks, tn), lambda i, j: (0, j)),
                  pl.BlockSpec((tm, tn), lambda i, j: (i, off_ga // tn + j)),
                  pl.BlockSpec((tm, tn), lambda i, j: (i, off_gs // tn + j))],
        out_specs=pl.BlockSpec((tm, tn), lambda i, j: (i, j)),
        out_shape=jax.ShapeDtypeStruct((m, n), BF16),
        compiler_params=_cp("parallel", "arbitrary"), name="gated_merge")(att, yn, wa, ws, proj, proj)


def _outproj_body(u_ref, wo_ref, h_ref, wn_ref, h2_ref, hn_ref):
    h2 = h_ref[...].astype(F32) + _dot(u_ref[...], wo_ref[...])
    h2_ref[...] = h2
    ms = jnp.mean(h2 * h2, axis=-1, keepdims=True)
    hn_ref[...] = (h2 * lax.rsqrt(ms + EPS) * wn_ref[...]).astype(hn_ref.dtype)


def _outproj(u, wo, h, wn):
    m, d = h.shape
    tm = _pick(m, 512)
    return pl.pallas_call(
        _outproj_body, grid=(m // tm,),
        in_specs=[pl.BlockSpec((tm, d), lambda i: (i, 0)),
                  pl.BlockSpec((d, d), lambda i: (0, 0)),
                  pl.BlockSpec((tm, d), lambda i: (i, 0)),
                  pl.BlockSpec((1, d), lambda i: (0, 0))],
        out_specs=[pl.BlockSpec((tm, d), lambda i: (i, 0)), pl.BlockSpec((tm, d), lambda i: (i, 0))],
        out_shape=[jax.ShapeDtypeStruct((m, d), F32), jax.ShapeDtypeStruct((m, d), BF16)],
        compiler_params=_cp("parallel"), name="out_proj")(u, wo, h, wn.reshape(1, d))


def _ffn_in_body(x_ref, wg_ref, wu_ref, o_ref):
    x = x_ref[...]
    wg = wg_ref[...].astype(BF16)
    wu = wu_ref[...].astype(BF16)
    o_ref[...] = (_silu(_dot(x, wg)) * _dot(x, wu)).astype(o_ref.dtype)


def _ffn_in(x, w, f):
    m, d = x.shape
    tm = _pick(m, 2048)
    tn = _pick(f, 512, LANES)
    nf = f // tn
    return pl.pallas_call(
        _ffn_in_body, grid=(m // tm, nf),
        in_specs=[pl.BlockSpec((tm, d), lambda i, j: (i, 0)),
                  pl.BlockSpec((d, tn), lambda i, j: (0, j)),
                  pl.BlockSpec((d, tn), lambda i, j: (0, nf + j))],
        out_specs=pl.BlockSpec((tm, tn), lambda i, j: (i, j)),
        out_shape=jax.ShapeDtypeStruct((m, f), BF16),
        compiler_params=_cp("parallel", "arbitrary"), name="ffn_in")(x, w, w)


def _ffn_out_body(a_ref, wd_ref, h_ref, o_ref):
    o_ref[...] = h_ref[...] + _dot(a_ref[...], wd_ref[...])


def _ffn_out(a, wd, h):
    m, f = a.shape
    d = wd.shape[1]
    tm = _pick(m, 1024)
    tn = _pick(d, 512, LANES)
    return pl.pallas_call(
        _ffn_out_body, grid=(m // tm, d // tn),
        in_specs=[pl.BlockSpec((tm, f), lambda i, j: (i, 0)),
                  pl.BlockSpec((f, tn), lambda i, j: (0, j)),
                  pl.BlockSpec((tm, tn), lambda i, j: (i, j))],
        out_specs=pl.BlockSpec((tm, tn), lambda i, j: (i, j)),
        out_shape=jax.ShapeDtypeStruct((m, d), F32),
        compiler_params=_cp("parallel", "arbitrary"), name="ffn_out")(a, wd, h)


def _spre_body(x_ref, s0_ref, s1_ref, s2_ref, cw_ref, cb_ref, xc_ref, xt_ref):
    x = x_ref[...].astype(F32)
    out = cb_ref[...] + cw_ref[0:1, :] * s0_ref[...] + cw_ref[1:2, :] * s1_ref[...] \
        + cw_ref[2:3, :] * s2_ref[...] + cw_ref[3:4, :] * x
    xc = _silu(out)
    xc_ref[...] = xc
    xt_ref[...] = xc.T.astype(xt_ref.dtype)


def _spre(proj, off_xbc, sconv, conv_w, conv_b, *, db, cc):
    tn = _pick(cc, 2048, LANES)
    nb = cc // tn
    return pl.pallas_call(
        _spre_body, grid=(nb,),
        in_specs=[pl.BlockSpec((db, tn), lambda j: (0, off_xbc // tn + j)),
                  pl.BlockSpec((db, tn), lambda j: (0, j)),
                  pl.BlockSpec((db, tn), lambda j: (0, nb + j)),
                  pl.BlockSpec((db, tn), lambda j: (0, 2 * nb + j)),
                  pl.BlockSpec((4, tn), lambda j: (0, j)),
                  pl.BlockSpec((1, tn), lambda j: (0, j))],
        out_specs=[pl.BlockSpec((db, tn), lambda j: (0, j)), pl.BlockSpec((tn, db), lambda j: (j, 0))],
        out_shape=[jax.ShapeDtypeStruct((db, cc), F32), jax.ShapeDtypeStruct((cc, db), BF16)],
        compiler_params=_cp("parallel"), name="sample_conv")(proj, sconv, sconv, sconv, conv_w, conv_b)


def _sdt_body(dk_ref, dtb_ref, alog_ref, dt_ref, dec_ref):
    x = dk_ref[...].astype(F32) + dtb_ref[...]
    dt = jnp.maximum(x, 0.0) + jnp.log(1.0 + jnp.exp(-jnp.abs(x)))
    dt_ref[...] = dt
    dec_ref[...] = jnp.exp(dt * (-jnp.exp(alog_ref[...])))


def _sdt(tail, off_dt, dtb, alog, *, db):
    blk = pl.BlockSpec((db, LANES), lambda i: (0, 0))
    return pl.pallas_call(
        _sdt_body, grid=(1,),
        in_specs=[pl.BlockSpec((db, LANES), lambda i: (0, off_dt // LANES)),
                  pl.BlockSpec((1, LANES), lambda i: (0, 0)), pl.BlockSpec((1, LANES), lambda i: (0, 0))],
        out_specs=[blk, blk],
        out_shape=[jax.ShapeDtypeStruct((db, LANES), F32), jax.ShapeDtypeStruct((db, LANES), F32)],
        compiler_params=_cp("arbitrary"), name="sample_dt")(tail, dtb, alog)


def _sstate_body(dec_ref, dts_ref, st_ref, xt_ref, b_ref, c_ref, sto_ref, yo_ref, *, H, P, N, G, db):
    b = pl.program_id(0)
    e = H // G
    di = H * P
    bb = b_ref[0]
    h0 = st_ref[0].reshape(di, N)
    y = _dot_nt(c_ref[0].astype(BF16), h0.astype(BF16))
    grow = lax.broadcasted_iota(jnp.int32, (G, di), 0)
    gcol = lax.broadcasted_iota(jnp.int32, (G, di), 1) // (e * P)
    yo_ref[0] = jnp.sum(jnp.where(grow == gcol, y, 0.0), axis=0, keepdims=True)
    onehot = lax.broadcasted_iota(jnp.int32, (db, N), 0) == b
    for g in range(G):
        mg = jnp.where(onehot, bb[g:g + 1, :], 0.0).astype(BF16)
        u = _dot(xt_ref[g * e * P:(g + 1) * e * P, :], mg)
        for hh in range(e):
            h = g * e + hh
            sto_ref[0, h] = st_ref[0, h] * dec_ref[b, h] + dts_ref[b, h] * u[hh * P:(hh + 1) * P, :]


def _sstate(dec, dts, state, xt, bm, cm, *, H, P, N, G):
    db = state.shape[0]
    di = H * P
    body = functools.partial(_sstate_body, H=H, P=P, N=N, G=G, db=db)
    smem = pl.BlockSpec(memory_space=pltpu.SMEM)
    return pl.pallas_call(
        body, grid=(db,),
        in_specs=[smem, smem,
                  pl.BlockSpec((1, H, P, N), lambda b: (b, 0, 0, 0)),
                  pl.BlockSpec((di, db), lambda b: (0, 0)),
                  pl.BlockSpec((1, G, N), lambda b: (b, 0, 0)),
                  pl.BlockSpec((1, G, N), lambda b: (b, 0, 0))],
        out_specs=[pl.BlockSpec((1, H, P, N), lambda b: (b, 0, 0, 0)),
                   pl.BlockSpec((1, 1, di), lambda b: (b, 0, 0))],
        out_shape=[jax.ShapeDtypeStruct((db, H, P, N), F32), jax.ShapeDtypeStruct((db, 1, di), F32)],
        compiler_params=_cp("parallel"), name="sample_state")(dec, dts, state, xt, bm, cm)


def _spost_body(xs_ref, b_ref, c_ref, yo_ref, dt_ref, dec_ref, z_ref, dsk_ref, wn_ref, o_ref, *, H, P, N, G):
    e = H // G
    di = H * P
    xs = xs_ref[...]
    hrow = lax.broadcasted_iota(jnp.int32, (LANES, di), 0)
    hcol = lax.broadcasted_iota(jnp.int32, (LANES, di), 1) // P
    expand = (hrow == hcol).astype(BF16)

    def rep(v):
        hi, mid, lo = _split3(v)
        return _dot(hi, expand) + _dot(mid, expand) + _dot(lo, expand)

    dt_rep = rep(dt_ref[...])
    dec_rep = rep(dec_ref[...])
    prod = b_ref[...] * c_ref[...]
    gw = e * P
    outs = []
    for g in range(G):
        cbg = jnp.sum(prod[:, g * N:(g + 1) * N], axis=-1, keepdims=True)
        sl = slice(g * gw, (g + 1) * gw)
        y = cbg * dt_rep[:, sl] * xs[:, sl] + dec_rep[:, sl] * yo_ref[:, sl] + dsk_ref[:, sl] * xs[:, sl]
        y = y * _silu(z_ref[:, sl].astype(F32))
        ms = jnp.mean(y * y, axis=-1, keepdims=True)
        outs.append(y * lax.rsqrt(ms + EPS))
    o_ref[...] = (jnp.concatenate(outs, axis=1) * wn_ref[...]).astype(o_ref.dtype)


def _spost(xc, yoff, dt, dec, proj, off_z, dsk, wn, *, H, P, N, G):
    db = xc.shape[0]
    di = H * P
    gn = G * N
    body = functools.partial(_spost_body, H=H, P=P, N=N, G=G)
    one = lambda i: (0, 0)
    return pl.pallas_call(
        body, grid=(1,),
        in_specs=[pl.BlockSpec((db, di), one),
                  pl.BlockSpec((db, gn), lambda i: (0, di // gn)),
                  pl.BlockSpec((db, gn), lambda i: (0, di // gn + 1)),
                  pl.BlockSpec((db, di), one),
                  pl.BlockSpec((db, LANES), one), pl.BlockSpec((db, LANES), one),
                  pl.BlockSpec((db, di), lambda i: (0, off_z // di)),
                  pl.BlockSpec((1, di), one), pl.BlockSpec((1, di), one)],
        out_specs=pl.BlockSpec((db, di), one),
        out_shape=jax.ShapeDtypeStruct((db, di), BF16),
        compiler_params=_cp("arbitrary"), name="sample_ssm_post")(xc, xc, xc, yoff, dt, dec, proj, dsk, wn)


def _sq_body(qn_ref, qr_ref, cos_ref, sin_ref, wn_ref, wr_ref, kn_ref, wk_ref, qa_ref, qo_ref, *, hm, nope, rope, scale):
    cos2 = cos_ref[...]
    sin2 = sin_ref[...]
    for h in range(hm):
        n = qn_ref[:, h * nope:(h + 1) * nope].astype(F32)
        r = qr_ref[:, h * rope:(h + 1) * rope].astype(F32)
        ms = (jnp.sum(n * n, axis=-1, keepdims=True) + jnp.sum(r * r, axis=-1, keepdims=True)) / (nope + rope)
        inv = lax.rsqrt(ms + EPS)
        rr = _rope(r * inv * wr_ref[...], cos2, sin2)
        qn = (n * inv * wn_ref[...] * kn_ref[...]).astype(BF16)
        qa_ref[h] = (_dot_nt(qn, wk_ref[h]) * scale).astype(qa_ref.dtype)
        qo_ref[h] = (rr * scale).astype(qo_ref.dtype)


def _sq(proj, off_qn, off_qr, cos2, sin2, wn, wr, kn_w, wk_h, *, hm, nope, rope, lora, scale, db):
    body = functools.partial(_sq_body, hm=hm, nope=nope, rope=rope, scale=scale)
    one = lambda i: (0, 0)
    return pl.pallas_call(
        body, grid=(1,),
        in_specs=[pl.BlockSpec((db, hm * nope), lambda i: (0, off_qn // (hm * nope))),
                  pl.BlockSpec((db, hm * rope), lambda i: (0, off_qr // (hm * rope))),
                  pl.BlockSpec((db, rope), one), pl.BlockSpec((db, rope), one),
                  pl.BlockSpec((1, nope), one), pl.BlockSpec((1, rope), one), pl.BlockSpec((1, nope), one),
                  pl.BlockSpec((hm, lora, nope), lambda i: (0, 0, 0))],
        out_specs=[pl.BlockSpec((hm, db, lora), lambda i: (0, 0, 0)), pl.BlockSpec((hm, db, rope), lambda i: (0, 0, 0))],
        out_shape=[jax.ShapeDtypeStruct((hm, db, lora), BF16), jax.ShapeDtypeStruct((hm, db, rope), BF16)],
        compiler_params=_cp("arbitrary"), name="sample_q_absorb")(proj, proj, cos2, sin2, wn, wr, kn_w, wk_h)


def _decode_body(pt_ref, qa_ref, qr_ref, cn_ref, kn_ref, sn_ref, ckv_hbm, kpe_hbm, ks_hbm, o_ref,
                 m_s, l_s, acc_s, cst, kst, sst, sems, *, G, PS, U):
    bu = pl.program_id(0)
    j = pl.program_id(1)
    nj = pl.num_programs(1)
    step = bu * nj + j
    slot = lax.rem(step, 2)

    def page_copies(bb, jj, sl):
        copies = []
        for u in range(U):
            for g in range(G):
                pg = pt_ref[bb * U + u, jj * G + g]
                keys = pl.ds(g * PS, PS)
                copies.append(pltpu.make_async_copy(ckv_hbm.at[0, pg], cst.at[sl, u, keys, :], sems.at[sl, 0]))
                copies.append(pltpu.make_async_copy(kpe_hbm.at[0, pg], kst.at[sl, u, :, keys], sems.at[sl, 1]))
                copies.append(pltpu.make_async_copy(ks_hbm.at[0, pg], sst.at[sl, u, :, keys], sems.at[sl, 2]))
        return copies

    @pl.when(step == 0)
    def _():
        for cp in page_copies(bu, j, slot):
            cp.start()

    @pl.when(step + 1 < pl.num_programs(0) * nj)
    def _():
        wrap = j == nj - 1
        for cp in page_copies(jnp.where(wrap, bu + 1, bu), jnp.where(wrap, 0, j + 1), 1 - slot):
            cp.start()

    for cp in page_copies(bu, j, slot):
        cp.wait()

    @pl.when(j == 0)
    def _():
        m_s[...] = jnp.full(m_s.shape, -jnp.inf, F32)
        l_s[...] = jnp.zeros(l_s.shape, F32)
        acc_s[...] = jnp.zeros(acc_s.shape, F32)

    for u in range(U):
        cb = cst[slot, u].astype(BF16)
        qa = qa_ref[u]
        qr = qr_ref[u]
        sc = (_dot_nt(qa, cb) + _dot(qr, kst[slot, u].astype(BF16))) * sst[slot, u]
        m_old = m_s[u, :, 0:1]
        m_new = jnp.maximum(m_old, jnp.max(sc, axis=1, keepdims=True))
        corr = jnp.exp(m_old - m_new)
        p = jnp.exp(sc - m_new)
        l_new = l_s[u, :, 0:1] * corr + jnp.sum(p, axis=1, keepdims=True)
        acc = acc_s[u] * corr + _dot(p.astype(BF16), cb)
        m_s[u] = jnp.broadcast_to(m_new, m_s.shape[1:])
        l_s[u] = jnp.broadcast_to(l_new, l_s.shape[1:])
        acc_s[u] = acc

    @pl.when(j == nj - 1)
    def _():
        for u in range(U):
            cn = cn_ref[u].astype(F32)
            kn = kn_ref[u].astype(F32)
            scn = (jnp.sum(qa_ref[u].astype(F32) * cn, axis=1, keepdims=True)
                   + jnp.sum(qr_ref[u].astype(F32) * kn, axis=1, keepdims=True)) * sn_ref[u]
            m_new = m_s[u, :, 0:1]
            m_f = jnp.maximum(m_new, scn)
            corr2 = jnp.exp(m_new - m_f)
            pn = jnp.exp(scn - m_f)
            l_f = l_s[u, :, 0:1] * corr2 + pn
            o_ref[u] = (acc_s[u] * corr2 + pn * cn) / l_f


def _decode(page_table, qa, qr, ckv_n, kpe_n, ks_n, cache_ckv, cache_kpe_t, cache_ks_t, *, G, U):
    db, hm, lora = qa.shape
    rope = qr.shape[-1]
    ps = cache_ckv.shape[2]
    npg = page_table.shape[1]
    G = _pick(npg, G, 1)
    U = _pick(db, U, 1)
    body = functools.partial(_decode_body, G=G, PS=ps, U=U)
    per_b = lambda b, j, pt: (b, 0, 0)
    hbm = pl.BlockSpec(memory_space=pl.ANY)
    grid_spec = pltpu.PrefetchScalarGridSpec(
        num_scalar_prefetch=1, grid=(db // U, npg // G),
        in_specs=[pl.BlockSpec((U, hm, lora), per_b), pl.BlockSpec((U, hm, rope), per_b),
                  pl.BlockSpec((U, 1, lora), per_b), pl.BlockSpec((U, 1, rope), per_b),
                  pl.BlockSpec((U, hm, 1), per_b), hbm, hbm, hbm],
        out_specs=pl.BlockSpec((U, hm, lora), per_b),
        scratch_shapes=[pltpu.VMEM((U, hm, LANES), F32), pltpu.VMEM((U, hm, LANES), F32),
                        pltpu.VMEM((U, hm, lora), F32),
                        pltpu.VMEM((2, U, G * ps, lora), F32), pltpu.VMEM((2, U, rope, G * ps), F32),
                        pltpu.VMEM((2, U, hm, G * ps), F32), pltpu.SemaphoreType.DMA((2, 3))])
    return pl.pallas_call(
        body, grid_spec=grid_spec, out_shape=jax.ShapeDtypeStruct((db, hm, lora), F32),
        compiler_params=_cp("arbitrary", "arbitrary"), name="mla_decode")(
            page_table, qa, qr, ckv_n, kpe_n, ks_n, cache_ckv, cache_kpe_t, cache_ks_t)


def _satt_body(o_ref, wv_ref, a_ref, *, hm, vd):
    for h in range(hm):
        a_ref[:, h * vd:(h + 1) * vd] = _dot(o_ref[h].astype(BF16), wv_ref[h]).astype(a_ref.dtype)


def _satt(o_t, wv_h):
    hm, db, lora = o_t.shape
    vd = wv_h.shape[-1]
    body = functools.partial(_satt_body, hm=hm, vd=vd)
    return pl.pallas_call(
        body, grid=(1,),
        in_specs=[pl.BlockSpec((hm, db, lora), lambda i: (0, 0, 0)), pl.BlockSpec((hm, lora, vd), lambda i: (0, 0, 0))],
        out_specs=pl.BlockSpec((db, hm * vd), lambda i: (0, 0)),
        out_shape=jax.ShapeDtypeStruct((db, hm * vd), BF16),
        compiler_params=_cp("arbitrary"), name="sample_v_up")(o_t, wv_h)


def _rope_tables(pos, rope):
    half = rope // 2
    inv = ROPE_THETA ** (-jnp.arange(half, dtype=jnp.float32) / half)
    ang = pos.astype(jnp.float32)[:, None] * inv[None, :]
    cos, sin = jnp.cos(ang), jnp.sin(ang)
    return jnp.concatenate([cos, cos], axis=-1), jnp.concatenate([-sin, sin], axis=-1)


def _pad_lanes(v):
    return jnp.pad(v.astype(F32), (0, LANES - v.shape[0])).reshape(1, LANES)


def kernel(x_prompt, x_sample, cache_ckv, cache_kpe, cache_kscale, page_table, state_ssm, state_conv, meta_tokens,
           w_norm_mix, w_in, conv_w, conv_b, dt_bias, a_log, d_skip, w_ssm_norm, w_ssm_out, w_kv_norm, q_norm, k_norm,
           w_kv_up, w_attn_out, w_out, w_norm_ffn, w_ffn_in, w_ffn_out):
    batch, seq, d = x_prompt.shape
    db, ts, _ = x_sample.shape
    depth = w_in.shape[0]
    assert depth == 1 and ts == 1, "single layer, one decode token per sequence"
    nm = meta_tokens.shape[0]
    H, P, N = state_ssm.shape[2:]
    di = H * P
    cc = conv_w.shape[2]
    G = (cc - di) // (2 * N)
    gn = G * N
    hm = w_kv_up.shape[2]
    qkh = q_norm.shape[1]
    rope = cache_kpe.shape[3]
    nope = qkh - rope
    vd = w_kv_up.shape[3] - nope
    lora = w_kv_up.shape[1]
    ps = cache_ckv.shape[2]
    past = page_table.shape[1] * ps
    f = w_ffn_out.shape[1]
    scale = qkh ** -0.5
    assert H + rope == LANES and seq % CHUNK == 0 and nm <= CHUNK and P * 2 == LANES

    w0 = w_in[0]
    cuts, acc = [], 0
    for c in (di, cc, H, hm * qkh, lora, rope, d, d):
        cuts.append((acc, acc + c))
        acc += c
    w_z, w_xbc, w_dt, w_q, w_ckv, w_kr, w_ga, w_gs = (w0[:, a:b] for a, b in cuts)
    w_q = w_q.reshape(d, hm, qkh)
    w_qn = w_q[:, :, :nope].reshape(d, hm * nope)
    w_qr = w_q[:, :, nope:].reshape(d, hm * rope)
    w_main = jnp.concatenate([w_z, w_xbc, w_qn, w_qr, w_ga, w_gs], axis=1).astype(BF16)
    w_tail = jnp.concatenate([w_ckv, w_dt, w_kr], axis=1).astype(BF16)
    off_z, off_xs, off_b, off_c = 0, di, 2 * di, 2 * di + gn
    off_qn = di + cc
    off_qr = off_qn + hm * nope
    off_ga = off_qr + hm * rope
    off_gs = off_ga + d
    off_dt = lora

    wkv = w_kv_up[0]
    wk_flat = wkv[:, :, :nope].reshape(lora, hm * nope).astype(BF16)
    wv_flat = wkv[:, :, nope:].reshape(lora, hm * vd).astype(BF16)
    wk_h = jnp.transpose(wkv[:, :, :nope], (1, 0, 2)).astype(BF16)
    wv_h = jnp.transpose(wkv[:, :, nope:], (1, 0, 2)).astype(BF16)
    wa = w_attn_out[0].astype(BF16)
    ws = w_ssm_out[0].astype(BF16)
    wo = w_out[0].astype(BF16)
    wf_in = w_ffn_in[0]
    wf_out = w_ffn_out[0].astype(BF16)
    qn_w = q_norm[0, :nope].reshape(1, nope)
    qr_w = q_norm[0, nope:].reshape(1, rope)
    kn_w = k_norm[0, :nope].reshape(1, nope)
    kr_w = k_norm[0, nope:].reshape(1, rope)
    wc = w_kv_norm[0].reshape(1, lora)
    cw, cbias = conv_w[0], conv_b[0].reshape(1, cc)
    cwx, cwb, cwc = cw[:, :di], cw[:, di:di + gn], cw[:, di + gn:]
    cbx, cbb, cbc = cbias[:, :di], cbias[:, di:di + gn], cbias[:, di + gn:]
    dtb = _pad_lanes(dt_bias[0])
    alog = _pad_lanes(a_log[0])
    dsk = jnp.repeat(d_skip[0].astype(F32), P).reshape(1, di)
    wsn = w_ssm_norm[0].reshape(1, di)

    cos_p, sin_p = _rope_tables(jnp.arange(nm + seq), rope)
    cos_s, sin_s = _rope_tables(past + jnp.arange(ts), rope)
    small = db + nm
    cos_small = jnp.concatenate([jnp.broadcast_to(cos_s, (db, rope)), cos_p[:nm]], axis=0)
    sin_small = jnp.concatenate([jnp.broadcast_to(sin_s, (db, rope)), sin_p[:nm]], axis=0)

    xp = x_prompt.reshape(batch * seq, d)
    hn_p = _rmsnorm(xp, w_norm_mix[0])
    proj_p = _matmul(hn_p, w_main, BF16, tm_pref=2048, name="in_proj")
    tail_p = _matmul(hn_p, w_tail, F32, name="in_proj_tail")
    xsm = jnp.concatenate([x_sample.reshape(db, d), meta_tokens.astype(x_prompt.dtype)], axis=0)
    hn_s = _rmsnorm(xsm, w_norm_mix[0])
    proj_s = _matmul(hn_s, w_main, F32, name="in_proj_small")
    tail_s = _matmul(hn_s, w_tail, F32, name="in_proj_tail_small")

    lead = CHUNK - nm
    pad = lambda a: jnp.pad(a, ((lead, 0), (0, 0)))
    proj_m, tail_m = pad(proj_s[db:]), pad(tail_s[db:])
    zeros_h = jnp.zeros((N, di), F32)
    zx, zb = jnp.zeros((HALO, di), F32), jnp.zeros((HALO, gn), F32)
    ssd_w = (cwx, cbx, cwb, cbb, cwc, cbc, dtb, alog, dsk, wsn)
    _, _, ht_meta = _ssd(proj_m, tail_m, off_z, off_xs, off_b, off_c, off_dt, *ssd_w, zeros_h, zx, zb, zb,
                         batch=1, nch=1, P=P, N=N, G=G, H=H, lead=lead)
    halo = proj_m[CHUNK - HALO:]
    yn_p, st_p, _ = _ssd(proj_p, tail_p, off_z, off_xs, off_b, off_c, off_dt, *ssd_w, ht_meta[0],
                         halo[:, off_xs:off_xs + di], halo[:, off_b:off_b + gn], halo[:, off_c:off_c + gn],
                         batch=batch, nch=seq // CHUNK, P=P, N=N, G=G, H=H)

    q_p = _prep_q(proj_p, off_qn, off_qr, cos_p[nm:], sin_p[nm:], qn_w, qr_w,
                  hm=hm, nope=nope, rope=rope, scale=scale, seq=seq)
    mla = dict(hm=hm, nope=nope, rope=rope, vd=vd, lora=lora, hs=H)
    ckv_x, kpe_x, ks_x, k_x, v_x = _prep_kv(tail_p, cos_p[nm:], sin_p[nm:], wc, wk_flat, wv_flat, kn_w, kr_w,
                                            seq=seq, **mla)
    ckv_sm, kpe_sm, ks_sm, k_sm, v_sm = _prep_kv(tail_s, cos_small, sin_small, wc, wk_flat, wv_flat, kn_w, kr_w,
                                                 seq=small, **mla)
    att_p = _flash(q_p.reshape(hm, batch, seq, qkh), k_x.reshape(hm, batch, seq, qkh),
                   v_x.reshape(hm, batch, seq, vd), k_sm[:, db:], v_sm[:, db:])

    u_p = _merge(att_p, yn_p, wa, ws, proj_p, off_ga, off_gs)
    h2_p, hn2_p = _outproj(u_p, wo, xp, w_norm_ffn[0])
    y_p = _ffn_out(_ffn_in(hn2_p, wf_in, f), wf_out, h2_p)

    sconv = state_conv[0].reshape(db, 3 * cc)
    xc_s, xt_s = _spre(proj_s, off_xs, sconv, cw, cbias, db=db, cc=cc)
    dt_s, dec_s = _sdt(tail_s, off_dt, dtb, alog, db=db)
    bm_s = xc_s[:, di:di + gn].reshape(db, G, N)
    cm_s = xc_s[:, di + gn:].reshape(db, G, N)
    st_s, yoff_s = _sstate(dec_s[:, :H], dt_s[:, :H], state_ssm[0], xt_s[:di], bm_s, cm_s, H=H, P=P, N=N, G=G)
    yn_s = _spost(xc_s, yoff_s.reshape(db, di), dt_s, dec_s, proj_s, off_z, dsk, wsn, H=H, P=P, N=N, G=G)

    qa_h, qr_h = _sq(proj_s, off_qn, off_qr, cos_small, sin_small, qn_w, qr_w, kn_w, wk_h,
                     hm=hm, nope=nope, rope=rope, lora=lora, scale=scale, db=db)
    qa_b = jnp.transpose(qa_h, (1, 0, 2))
    qr_b = jnp.transpose(qr_h, (1, 0, 2))
    o_s = _decode(page_table, qa_b, qr_b, ckv_sm[:db].reshape(db, 1, lora), kpe_sm[:db].reshape(db, 1, rope),
                  ks_sm[:db].reshape(db, hm, 1), cache_ckv, jnp.swapaxes(cache_kpe, 2, 3),
                  jnp.swapaxes(cache_kscale, 2, 3), G=64, U=1)
    att_s = _satt(jnp.transpose(o_s, (1, 0, 2)), wv_h)

    u_s = _merge(att_s, yn_s, wa, ws, proj_s, off_ga, off_gs)
    h2_s, hn2_s = _outproj(u_s, wo, x_sample.reshape(db, d), w_norm_ffn[0])
    y_s = _ffn_out(_ffn_in(hn2_s, wf_in, f), wf_out, h2_s)

    def with_meta(xpart, mpart):
        c = xpart.shape[-1]
        m = jnp.broadcast_to(mpart[None], (batch, nm, c))
        return jnp.concatenate([m, xpart.reshape(batch, seq, c)], axis=1)[None]

    conv_p = proj_p.reshape(batch, seq, -1)[:, seq - 3:, off_xs:off_xs + cc].astype(F32)[None]
    conv_s = jnp.concatenate([state_conv[0][:, 1:], proj_s[:db, None, off_xs:off_xs + cc]], axis=1)[None]
    return (y_p.reshape(batch, seq, d), y_s.reshape(db, ts, d),
            with_meta(ckv_x, ckv_sm[db:]), with_meta(kpe_x, kpe_sm[db:]), with_meta(ks_x, ks_sm[db:]),
            st_p.reshape(batch, H, P, N)[None], conv_p,
            ckv_sm[:db].reshape(db, ts, lora)[None], kpe_sm[:db].reshape(db, ts, rope)[None],
            ks_sm[:db].reshape(db, ts, hm)[None], st_s[None], conv_s)
```
